```python
import jax, jax.numpy as jnp
from jax import lax
import numpy as np

D_MODEL = 1024
BATCH = 2
SEQ = 16384
DEPTH = 4
DEC_BATCH = 8
DEC_SEQ = 32
PAST_LEN = 4096

CHUNK = 64
D_MIX = D_MODEL
D_A = D_MIX // 2
N_A_HEADS = 4
A_HEAD_DIM = D_A // N_A_HEADS
GMLP_CHUNK = 128
D_B = D_MIX // 4
POOL_WINDOWS = (2, 4, 8, 16)
N_POOL_GROUPS = len(POOL_WINDOWS)
POOL_GROUP = D_B // N_POOL_GROUPS
POOL_HIST = max(POOL_WINDOWS) - 1
D_C = D_MIX - D_A - D_B
CONV_W = 3
CONV_HIST = CONV_W - 1
D_IN = 2 * D_A + D_B + 3 * D_C
D_FF = 11 * D_MODEL // 4
EPS = 1e-6

kernel_name = 'hybrid_stream_gmlp_pool_shortconv'


def _rmsnorm(x, g):
    xf = x.astype(jnp.float32)
    y = xf * lax.rsqrt(jnp.mean(xf * xf, axis=-1, keepdims=True) + EPS)
    return (y * g.astype(jnp.float32)).astype(x.dtype)


def _causal_dwconv3(ext, w):
    return w[0] * ext[:, :-2] + w[1] * ext[:, 1:-1] + w[2] * ext[:, 2:]


def _chunk_spatial_gate(u, v, w_s, b_s):
    b, T, _ = v.shape
    L = min(T, GMLP_CHUNK)
    n = T // L
    mask = jnp.tril(jnp.ones((L, L), dtype=w_s.dtype))
    w = w_s[:, :L, :L] * mask
    vh = v.reshape(b, n, L, N_A_HEADS, A_HEAD_DIM)
    s = jnp.einsum('hqk,bnkhd->bnqhd', w, vh) + b_s[:, :L].T[:, :, None]
    return u * s.reshape(b, T, D_A)


def _multiscale_pool(p_ext, pos0, w_pool, scale):
    b, Lx, _ = p_ext.shape
    T = Lx - POOL_HIST
    pf = p_ext.astype(jnp.float32)
    cs = jnp.concatenate([jnp.zeros((b, 1, D_B), jnp.float32), lax.cumsum(pf, axis=1)], axis=1)
    end = cs[:, POOL_HIST + 1:]
    pos = pos0 + jnp.arange(T)
    means = []
    for gi, w in enumerate(POOL_WINDOWS):
        sl = slice(gi * POOL_GROUP, (gi + 1) * POOL_GROUP)
        start = cs[:, POOL_HIST + 1 - w: POOL_HIST + 1 - w + T, sl]
        cnt = jnp.minimum(pos + 1, w).astype(jnp.float32)[None, :, None]
        means.append((end[..., sl] - start) / cnt)
    d = (jnp.concatenate(means, axis=-1) - pf[:, POOL_HIST:]).reshape(b, T, N_POOL_GROUPS, POOL_GROUP)
    y = jnp.einsum('btgc,gcd->btgd', d, w_pool.astype(jnp.float32)).reshape(b, T, D_B)
    return (y * scale.astype(jnp.float32)).astype(p_ext.dtype)


def _layer(x, pool_hist, conv_hist, ffn_hist, pos0, g1, w_in, w_s, b_s, w_pool, pool_scale,
           w_conv, w_out, g2, w_up, w_fconv, b_fconv, w_down):
    h = _rmsnorm(x, g1)
    z = h @ w_in
    cuts = [D_A, 2 * D_A, 2 * D_A + D_B, 2 * D_A + D_B + D_C, 2 * D_A + D_B + 2 * D_C]
    u_a, v_a, p_b, gate_b, gate_c, h_c = jnp.split(z, cuts, axis=-1)
    y_a = _chunk_spatial_gate(u_a, v_a, w_s, b_s)
    p_ext = jnp.concatenate([pool_hist, p_b], axis=1)
    y_b = _multiscale_pool(p_ext, pos0, w_pool, pool_scale)
    q_ext = jnp.concatenate([conv_hist, gate_c * h_c], axis=1)
    y_c = gate_b * _causal_dwconv3(q_ext, w_conv)
    x = x + jnp.concatenate([y_a, y_b, y_c], axis=-1) @ w_out
    h2 = _rmsnorm(x, g2)
    up_ext = jnp.concatenate([ffn_hist, h2 @ w_up], axis=1)
    upc = _causal_dwconv3(up_ext, w_fconv) + b_fconv
    g, a = jnp.split(upc, 2, axis=-1)
    x = x + (jax.nn.silu(g) * a) @ w_down
    return x, p_ext[:, -POOL_HIST:], q_ext[:, -CONV_HIST:], up_ext[:, -CONV_HIST:], v_a


def setup_inputs(seed: int = 0) -> dict:
    key = jax.random.key(seed)
    ks = jax.random.split(key, 24)
    f32 = jnp.float32
    nrm = lambda k, s, sc: jax.random.normal(k, s, f32) * sc
    return {
        'x_prompt': nrm(ks[0], (BATCH, SEQ, D_MODEL), 1.0),
        'x_sample': nrm(ks[1], (DEC_BATCH, DEC_SEQ, D_MODEL), 1.0),
        'state_pool': nrm(ks[2], (DEPTH, DEC_BATCH, POOL_HIST, D_B), 1.0),
        'state_conv': nrm(ks[3], (DEPTH, DEC_BATCH, CONV_HIST, D_C), 1.0),
        'state_ffn_conv': nrm(ks[4], (DEPTH, DEC_BATCH, CONV_HIST, 2 * D_FF), 1.0),
        'norm1_g': 1.0 + nrm(ks[5], (DEPTH, D_MODEL), 0.02),
        'w_in': nrm(ks[6], (DEPTH, D_MODEL, D_IN), D_MODEL ** -0.5),
        'w_s': nrm(ks[7], (DEPTH, N_A_HEADS, GMLP_CHUNK, GMLP_CHUNK), 0.5 * GMLP_CHUNK ** -0.5),
        'b_s': 1.0 + nrm(ks[8], (DEPTH, N_A_HEADS, GMLP_CHUNK), 0.1),
        'w_pool': nrm(ks[9], (DEPTH, N_POOL_GROUPS, POOL_GROUP, POOL_GROUP), POOL_GROUP ** -0.5),
        'pool_scale': 1.0 + nrm(ks[10], (DEPTH, D_B), 0.1),
        'w_conv': nrm(ks[11], (DEPTH, CONV_W, D_C), CONV_W ** -0.5),
        'w_out': nrm(ks[12], (DEPTH, D_MIX, D_MODEL), D_MIX ** -0.5),
        'norm2_g': 1.0 + nrm(ks[13], (DEPTH, D_MODEL), 0.02),
        'w_up': nrm(ks[14], (DEPTH, D_MODEL, 2 * D_FF), D_MODEL ** -0.5),
        'w_fconv': nrm(ks[15], (DEPTH, CONV_W, 2 * D_FF), CONV_W ** -0.5),
        'b_fconv': nrm(ks[16], (DEPTH, 2 * D_FF), 0.02),
        'w_down': nrm(ks[17], (DEPTH, D_FF, D_MODEL), D_FF ** -0.5),
        'final_g': 1.0 + nrm(ks[18], (D_MODEL,), 0.02),
    }


def reference(x_prompt, x_sample, state_pool, state_conv, state_ffn_conv, norm1_g, w_in, w_s, b_s,
              w_pool, pool_scale, w_conv, w_out, norm2_g, w_up, w_fconv, b_fconv, w_down, final_g):
    xp, xs = x_prompt, x_sample
    bp = xp.shape[0]
    zero_pool = jnp.zeros((bp, POOL_HIST, D_B), xp.dtype)
    zero_conv = jnp.zeros((bp, CONV_HIST, D_C), xp.dtype)
    zero_ffn = jnp.zeros((bp, CONV_HIST, 2 * D_FF), xp.dtype)
    pool_p, conv_p, ffn_p = [], [], []
    pool_s, conv_s, ffn_s, chunk_v_s = [], [], [], []
    for l in range(DEPTH):
        params = (norm1_g[l], w_in[l], w_s[l], b_s[l], w_pool[l], pool_scale[l], w_conv[l], w_out[l],
                  norm2_g[l], w_up[l], w_fconv[l], b_fconv[l], w_down[l])
        xp, pp, cp, fp, _ = _layer(xp, zero_pool, zero_conv, zero_ffn, 0, *params)
        xs, ps, cs, fs, vs = _layer(xs, state_pool[l], state_conv[l], state_ffn_conv[l], PAST_LEN, *params)
        pool_p.append(pp); conv_p.append(cp); ffn_p.append(fp)
        pool_s.append(ps); conv_s.append(cs); ffn_s.append(fs); chunk_v_s.append(vs)
    y_prompt = _rmsnorm(xp, final_g)
    y_sample = _rmsnorm(xs, final_g)
    return (y_prompt, y_sample, jnp.stack(pool_p), jnp.stack(conv_p), jnp.stack(ffn_p),
            jnp.stack(pool_s), jnp.stack(conv_s), jnp.stack(ffn_s), jnp.stack(chunk_v_s))
```

```python
import functools

import jax
import jax.numpy as jnp
from jax import lax
from jax.experimental import pallas as pl
from jax.experimental.pallas import tpu as pltpu

D_MODEL = 1024
D_A = 512
N_A_HEADS = 4
A_HEAD_DIM = 128
GMLP_CHUNK = 128
D_B = 256
POOL_WINDOWS = (2, 4, 8, 16)
POOL_GROUP = 64
POOL_HIST = 15
D_C = 256
CONV_HIST = 2
D_IN = 2 * D_A + D_B + 3 * D_C
D_FF = 11 * D_MODEL // 4
EPS = 1e-6
PAST_LEN = 4096

LANES = 128
SUBLANES = 8
POOL_HIST_PAD = 16
CONV_HIST_PAD = SUBLANES
FF_CHUNK = 256
PROMPT_TILE = 256
VMEM_LIMIT_BYTES = 56 * 1024 * 1024

_F32 = jnp.float32
_BF16 = jnp.bfloat16


def _rmsnorm(x, g):
    y = x * lax.rsqrt(jnp.mean(x * x, axis=-1, keepdims=True) + EPS)
    return y * g


def _shift_rows(e, k):
    return pltpu.roll(e, k, axis=0)


def _causal_conv3(cur, hist, w):
    rows = cur.shape[0]
    e = jnp.concatenate([hist, cur], axis=0)
    out = w[0:1] * _shift_rows(e, 2) + w[1:2] * _shift_rows(e, 1) + w[2:3] * e
    return out[CONV_HIST_PAD:], e[rows:rows + CONV_HIST_PAD]


def _pool_delta(p, hist, first_pos):
    rows = p.shape[0]
    e = jnp.concatenate([hist, p], axis=0)
    e0, e1 = e[:, :LANES], e[:, LANES:]
    s2_0 = e0 + _shift_rows(e0, 1)
    s4_0 = s2_0 + _shift_rows(s2_0, 2)
    s2_1 = e1 + _shift_rows(e1, 1)
    s4_1 = s2_1 + _shift_rows(s2_1, 2)
    s8_1 = s4_1 + _shift_rows(s4_1, 4)
    s16_1 = s8_1 + _shift_rows(s8_1, 8)
    lane = lax.broadcasted_iota(jnp.int32, (rows, LANES), 1)
    low = lane < POOL_GROUP
    sums = jnp.concatenate(
        [jnp.where(low, s2_0[POOL_HIST_PAD:], s4_0[POOL_HIST_PAD:]),
         jnp.where(low, s8_1[POOL_HIST_PAD:], s16_1[POOL_HIST_PAD:])], axis=1)
    lane_b = lax.broadcasted_iota(jnp.int32, (rows, D_B), 1)
    window = jnp.where(lane_b < POOL_GROUP, POOL_WINDOWS[0],
                       jnp.where(lane_b < 2 * POOL_GROUP, POOL_WINDOWS[1],
                                 jnp.where(lane_b < 3 * POOL_GROUP, POOL_WINDOWS[2], POOL_WINDOWS[3])))
    pos1 = lax.broadcasted_iota(jnp.int32, (rows, D_B), 0) + (first_pos + 1)
    cnt = jnp.minimum(pos1, window).astype(_F32)
    return sums / cnt - p, e[rows:rows + POOL_HIST_PAD]


def _layer_kernel(x_ref, poolh_ref, convh_ref, ffnh_ref, g1_ref, w_in_ref, w_s_ref, bias_ref, w_pool_ref,
                  pscale_ref, w_conv_ref, w_out_ref, g2_ref, w_up_ref, w_fconv_ref, b_fconv_ref, w_down_ref,
                  gf_ref, *rest, n_streams, rows, pos0, final_norm, emit_v):
    if emit_v:
        y_ref, poolo_ref, convo_ref, ffno_ref, v_ref, pool_s, conv_s, ffn_s = rest
    else:
        y_ref, poolo_ref, convo_ref, ffno_ref, pool_s, conv_s, ffn_s = rest
    t = pl.program_id(1)

    @pl.when(t == 0)
    def _load_history():
        pool_s[...] = poolh_ref[...]
        conv_s[...] = convh_ref[...]
        ffn_s[...] = ffnh_ref[...]

    def stream_rows(a, s):
        return a[s * rows:(s + 1) * rows]

    x = x_ref[...]
    h = _rmsnorm(x, g1_ref[...]).astype(_BF16)
    z = jnp.dot(h, w_in_ref[...], preferred_element_type=_F32)
    u = z[:, 0:D_A]
    v = z[:, D_A:2 * D_A]
    p = z[:, 2 * D_A:2 * D_A + D_B]
    gate_b = z[:, 2 * D_A + D_B:2 * D_A + D_B + D_C]
    gate_c = z[:, 2 * D_A + D_B + D_C:2 * D_A + D_B + 2 * D_C]
    h_c = z[:, 2 * D_A + D_B + 2 * D_C:]
    if emit_v:
        v_ref[...] = v

    chunk = min(rows, GMLP_CHUNK)
    n_chunks = n_streams * rows // chunk
    v_b = v.astype(_BF16)
    tril = (lax.broadcasted_iota(jnp.int32, (chunk, chunk), 0)
            >= lax.broadcasted_iota(jnp.int32, (chunk, chunk), 1)).astype(_F32)
    y_a_heads = []
    for hd in range(N_A_HEADS):
        cols = slice(hd * A_HEAD_DIM, (hd + 1) * A_HEAD_DIM)
        w_hd = (w_s_ref[hd, 0:chunk, 0:chunk] * tril).astype(_BF16)
        v_hd = jnp.concatenate([v_b[c * chunk:(c + 1) * chunk, cols] for c in range(n_chunks)], axis=1)
        s_hd = jnp.dot(w_hd, v_hd, preferred_element_type=_F32)
        b_hd = bias_ref[0:chunk, cols]
        s_rows = jnp.concatenate(
            [s_hd[:, c * A_HEAD_DIM:(c + 1) * A_HEAD_DIM] + b_hd for c in range(n_chunks)], axis=0)
        y_a_heads.append(u[:, cols] * s_rows)
    y_a = jnp.concatenate(y_a_heads, axis=1)

    first_pos = pos0 + t * rows
    d_parts = []
    for s in range(n_streams):
        d_s, hist = _pool_delta(stream_rows(p, s), pool_s[s], first_pos)
        pool_s[s] = hist
        poolo_ref[s] = hist
        d_parts.append(d_s)
    d = d_parts[0] if n_streams == 1 else jnp.concatenate(d_parts, axis=0)
    y_b = jnp.dot(d.astype(_BF16), w_pool_ref[...], preferred_element_type=_F32) * pscale_ref[...]

    q = gate_c * h_c
    c_parts = []
    for s in range(n_streams):
        c_s, hist = _causal_conv3(stream_rows(q, s), conv_s[s], w_conv_ref[...])
        conv_s[s] = hist
        convo_ref[s] = hist
        c_parts.append(c_s)
    y_c = gate_b * (c_parts[0] if n_streams == 1 else jnp.concatenate(c_parts, axis=0))

    y_mix = jnp.concatenate([y_a, y_b, y_c], axis=1).astype(_BF16)
    x1 = x + jnp.dot(y_mix, w_out_ref[...], preferred_element_type=_F32)

    h2 = _rmsnorm(x1, g2_ref[...]).astype(_BF16)
    acc = x1
    for c in range(D_FF // FF_CHUNK):
        halves = []
        for col0 in (c * FF_CHUNK, D_FF + c * FF_CHUNK):
            cols = slice(col0, col0 + FF_CHUNK)
            up = jnp.dot(h2, w_up_ref[:, cols], preferred_element_type=_F32)
            w_fc = w_fconv_ref[:, cols]
            parts = []
            for s in range(n_streams):
                c_s, hist = _causal_conv3(stream_rows(up, s), ffn_s[s, :, cols], w_fc)
                ffn_s[s, :, cols] = hist
                ffno_ref[s, :, cols] = hist
                parts.append(c_s)
            conv = parts[0] if n_streams == 1 else jnp.concatenate(parts, axis=0)
            halves.append(conv + b_fconv_ref[:, cols])
        gate, val = halves
        act = (gate * (1.0 / (1.0 + jnp.exp(-gate))) * val).astype(_BF16)
        acc = acc + jnp.dot(act, w_down_ref[c * FF_CHUNK:(c + 1) * FF_CHUNK, :], preferred_element_type=_F32)

    y_ref[...] = _rmsnorm(acc, gf_ref[...]) if final_norm else acc


def _layer_call(layer, x, pool_h, conv_h, ffn_h, params, *, n_groups, n_streams, rows, n_tiles, pos0,
                final_norm, emit_v):
    block_rows = n_streams * rows
    total_streams = n_groups * n_streams

    def rows_spec(width):
        return pl.BlockSpec((block_rows, width), lambda g, t: (g * n_tiles + t, 0))

    def state_spec(pad, width):
        return pl.BlockSpec((n_streams, pad, width), lambda g, t: (g, 0, 0))

    def layer_spec(shape):
        return pl.BlockSpec((None,) + shape, lambda g, t: (layer,) + (0,) * len(shape),
                            pipeline_mode=pl.Buffered(1))

    (g1, w_in, w_s, bias, w_pool, pscale, w_conv, w_out, g2, w_up, w_fconv, b_fconv, w_down, gf) = params
    in_specs = [
        rows_spec(D_MODEL),
        state_spec(POOL_HIST_PAD, D_B), state_spec(CONV_HIST_PAD, D_C), state_spec(CONV_HIST_PAD, 2 * D_FF),
        layer_spec((1, D_MODEL)), layer_spec((D_MODEL, D_IN)),
        layer_spec((N_A_HEADS, GMLP_CHUNK, GMLP_CHUNK)), layer_spec((GMLP_CHUNK, D_A)),
        layer_spec((D_B, D_B)), layer_spec((1, D_B)), layer_spec((3, D_C)),
        layer_spec((D_MODEL, D_MODEL)), layer_spec((1, D_MODEL)),
        layer_spec((D_MODEL, 2 * D_FF)), layer_spec((3, 2 * D_FF)), layer_spec((1, 2 * D_FF)),
        layer_spec((D_FF, D_MODEL)),
        pl.BlockSpec((1, D_MODEL), lambda g, t: (0, 0), pipeline_mode=pl.Buffered(1)),
    ]
    out_shape = [
        jax.ShapeDtypeStruct(x.shape, _F32),
        jax.ShapeDtypeStruct((total_streams, POOL_HIST_PAD, D_B), _F32),
        jax.ShapeDtypeStruct((total_streams, CONV_HIST_PAD, D_C), _F32),
        jax.ShapeDtypeStruct((total_streams, CONV_HIST_PAD, 2 * D_FF), _F32),
    ]
    out_specs = [rows_spec(D_MODEL), state_spec(POOL_HIST_PAD, D_B), state_spec(CONV_HIST_PAD, D_C),
                 state_spec(CONV_HIST_PAD, 2 * D_FF)]
    if emit_v:
        out_shape.append(jax.ShapeDtypeStruct((x.shape[0], D_A), _F32))
        out_specs.append(rows_spec(D_A))
    body = functools.partial(_layer_kernel, n_streams=n_streams, rows=rows, pos0=pos0,
                             final_norm=final_norm, emit_v=emit_v)
    return pl.pallas_call(
        body,
        grid=(n_groups, n_tiles),
        in_specs=in_specs,
        out_specs=out_specs,
        out_shape=out_shape,
        scratch_shapes=[pltpu.VMEM((n_streams, POOL_HIST_PAD, D_B), _F32),
                        pltpu.VMEM((n_streams, CONV_HIST_PAD, D_C), _F32),
                        pltpu.VMEM((n_streams, CONV_HIST_PAD, 2 * D_FF), _F32)],
        compiler_params=pltpu.CompilerParams(dimension_semantics=("arbitrary", "arbitrary"),
                                             vmem_limit_bytes=VMEM_LIMIT_BYTES),
        name=("sample" if emit_v else "prompt") + "_layer",
    )(x, pool_h, conv_h, ffn_h, g1, w_in, w_s, bias, w_pool, pscale, w_conv, w_out, g2, w_up, w_fconv,
      b_fconv, w_down, gf)


def _pad_front(a, pad):
    return jnp.pad(a, ((0, 0),) * (a.ndim - 2) + ((pad - a.shape[-2], 0), (0, 0)))


def kernel(x_prompt, x_sample, state_pool, state_conv, state_ffn_conv, norm1_g, w_in, w_s, b_s, w_pool,
           pool_scale, w_conv, w_out, norm2_g, w_up, w_fconv, b_fconv, w_down, final_g):
    depth = w_in.shape[0]
    batch, seq, _ = x_prompt.shape
    dec_batch, dec_seq, _ = x_sample.shape

    group_eye = jnp.eye(len(POOL_WINDOWS), dtype=bool)[None, :, None, :, None]
    w_pool_bd = jnp.where(group_eye, w_pool[:, :, :, None, :], 0.0).reshape(depth, D_B, D_B)
    bias = jnp.repeat(jnp.swapaxes(b_s, 1, 2), A_HEAD_DIM, axis=-1)
    params = (norm1_g[:, None, :], w_in.astype(_BF16), w_s, bias, w_pool_bd.astype(_BF16),
              pool_scale[:, None, :], w_conv, w_out.astype(_BF16), norm2_g[:, None, :], w_up.astype(_BF16),
              w_fconv, b_fconv[:, None, :], w_down.astype(_BF16), final_g[None, :])

    pool_hs = _pad_front(state_pool, POOL_HIST_PAD)
    conv_hs = _pad_front(state_conv, CONV_HIST_PAD)
    ffn_hs = _pad_front(state_ffn_conv, CONV_HIST_PAD)
    zero_pool = jnp.zeros((batch, POOL_HIST_PAD, D_B), _F32)
    zero_conv = jnp.zeros((batch, CONV_HIST_PAD, D_C), _F32)
    zero_ffn = jnp.zeros((batch, CONV_HIST_PAD, 2 * D_FF), _F32)

    xp = x_prompt.reshape(batch * seq, D_MODEL)
    xs = x_sample.reshape(dec_batch * dec_seq, D_MODEL)
    outs_p, outs_s = [], []
    for layer in range(depth):
        last = layer == depth - 1
        xp, *st_p = _layer_call(layer, xp, zero_pool, zero_conv, zero_ffn, params, n_groups=batch, n_streams=1,
                                rows=PROMPT_TILE, n_tiles=seq // PROMPT_TILE, pos0=0, final_norm=last,
                                emit_v=False)
        xs, *st_s = _layer_call(layer, xs, pool_hs[layer], conv_hs[layer], ffn_hs[layer], params, n_groups=1,
                                n_streams=dec_batch, rows=dec_seq, n_tiles=1, pos0=PAST_LEN, final_norm=last,
                                emit_v=True)
        outs_p.append(st_p)
        outs_s.append(st_s)

    def stacked(outs, i, keep):
        return jnp.stack([o[i] for o in outs])[:, :, -keep:, :]

    return (xp.reshape(batch, seq, D_MODEL), xs.reshape(dec_batch, dec_seq, D_MODEL),
            stacked(outs_p, 0, POOL_HIST), stacked(outs_p, 1, CONV_HIST), stacked(outs_p, 2, CONV_HIST),
            stacked(outs_s, 0, POOL_HIST), stacked(outs_s, 1, CONV_HIST), stacked(outs_s, 2, CONV_HIST),
            jnp.stack([o[3] for o in outs_s]).reshape(depth, dec_batch, dec_seq, D_A))
```

```python
import functools

import jax
import jax.numpy as jnp
from jax import lax
from jax.experimental import pallas as pl
from jax.experimental.pallas import tpu as pltpu

D_MODEL = 1024
D_A = 512
N_A_HEADS = 4
A_HEAD_DIM = 128
GMLP_CHUNK = 128
D_B = 256
POOL_WINDOWS = (2, 4, 8, 16)
POOL_GROUP = 64
POOL_HIST = 15
D_C = 256
CONV_HIST = 2
D_IN = 2 * D_A + D_B + 3 * D_C
D_FF = 11 * D_MODEL // 4
EPS = 1e-6
PAST_LEN = 4096

LANES = 128
SUBLANES = 8
POOL_HIST_PAD = 16
CONV_HIST_PAD = SUBLANES
FF_CHUNK = 256
PROMPT_TILE = 256
VMEM_LIMIT_BYTES = 56 * 1024 * 1024

_F32 = jnp.float32
_BF16 = jnp.bfloat16


def _rmsnorm(x, g):
    y = x * lax.rsqrt(jnp.mean(x * x, axis=-1, keepdims=True) + EPS)
    return y * g


def _shift_rows(e, k):
    return pltpu.roll(e, k, axis=0)


def _causal_conv3(cur, hist, w):
    rows = cur.shape[0]
    e = jnp.concatenate([hist, cur], axis=0)
    out = w[0:1] * _shift_rows(e, 2) + w[1:2] * _shift_rows(e, 1) + w[2:3] * e
    return out[CONV_HIST_PAD:], e[rows:rows + CONV_HIST_PAD]


def _causal_conv3_ref(buf, cur, w):
    rows = cur.shape[0]
    buf[CONV_HIST_PAD:, :] = cur
    out = (w[0:1] * buf[CONV_HIST_PAD - 2:CONV_HIST_PAD - 2 + rows, :]
           + w[1:2] * buf[CONV_HIST_PAD - 1:CONV_HIST_PAD - 1 + rows, :] + w[2:3] * cur)
    buf[0:CONV_HIST_PAD, :] = cur[rows - CONV_HIST_PAD:]
    return out


def _pool_delta(p, hist, first_pos):
    rows = p.shape[0]
    e = jnp.concatenate([hist, p], axis=0)
    e0, e1 = e[:, :LANES], e[:, LANES:]
    s2_0 = e0 + _shift_rows(e0, 1)
    s4_0 = s2_0 + _shift_rows(s2_0, 2)
    s2_1 = e1 + _shift_rows(e1, 1)
    s4_1 = s2_1 + _shift_rows(s2_1, 2)
    s8_1 = s4_1 + _shift_rows(s4_1, 4)
    s16_1 = s8_1 + _shift_rows(s8_1, 8)
    lane = lax.broadcasted_iota(jnp.int32, (rows, LANES), 1)
    low = lane < POOL_GROUP
    sums = jnp.concatenate(
        [jnp.where(low, s2_0[POOL_HIST_PAD:], s4_0[POOL_HIST_PAD:]),
         jnp.where(low, s8_1[POOL_HIST_PAD:], s16_1[POOL_HIST_PAD:])], axis=1)
    lane_b = lax.broadcasted_iota(jnp.int32, (rows, D_B), 1)
    window = jnp.where(lane_b < POOL_GROUP, POOL_WINDOWS[0],
                       jnp.where(lane_b < 2 * POOL_GROUP, POOL_WINDOWS[1],
                                 jnp.where(lane_b < 3 * POOL_GROUP, POOL_WINDOWS[2], POOL_WINDOWS[3])))
    pos1 = lax.broadcasted_iota(jnp.int32, (rows, D_B), 0) + (first_pos + 1)
    cnt = jnp.minimum(pos1, window).astype(_F32)
    return sums / cnt - p, e[rows:rows + POOL_HIST_PAD]


def _layer_kernel(x_ref, poolh_ref, convh_ref, ffnh_ref, g1_ref, w_in_ref, w_s_ref, bias_ref, w_pool_ref,
                  pscale_ref, w_conv_ref, w_out_ref, g2_ref, w_up_ref, w_fconv_ref, b_fconv_ref, w_down_ref,
                  gf_ref, *rest, n_streams, rows, pos0, final_norm, emit_v):
    if emit_v:
        y_ref, poolo_ref, convo_ref, ffno_ref, v_ref, pool_s, conv_s, ffn_s = rest
    else:
        y_ref, poolo_ref, convo_ref, ffno_ref, pool_s, conv_s, ffn_s = rest
    t = pl.program_id(1)

    @pl.when(t == 0)
    def _load_history():
        pool_s[...] = poolh_ref[...]
        conv_s[...] = convh_ref[...]
        for s in range(n_streams):
            for j in range(2 * D_FF // LANES):
                ffn_s[s, j, 0:CONV_HIST_PAD, :] = ffnh_ref[s, :, j * LANES:(j + 1) * LANES]

    def stream_rows(a, s):
        return a[s * rows:(s + 1) * rows]

    x = x_ref[...]
    h = _rmsnorm(x, g1_ref[...]).astype(_BF16)
    z = jnp.dot(h, w_in_ref[...], preferred_element_type=_F32)
    u = z[:, 0:D_A]
    v = z[:, D_A:2 * D_A]
    p = z[:, 2 * D_A:2 * D_A + D_B]
    gate_b = z[:, 2 * D_A + D_B:2 * D_A + D_B + D_C]
    gate_c = z[:, 2 * D_A + D_B + D_C:2 * D_A + D_B + 2 * D_C]
    h_c = z[:, 2 * D_A + D_B + 2 * D_C:]
    if emit_v:
        v_ref[...] = v

    chunk = min(rows, GMLP_CHUNK)
    n_chunks = n_streams * rows // chunk
    v_b = v.astype(_BF16)
    tril = (lax.broadcasted_iota(jnp.int32, (chunk, chunk), 0)
            >= lax.broadcasted_iota(jnp.int32, (chunk, chunk), 1)).astype(_F32)
    y_a_heads = []
    for hd in range(N_A_HEADS):
        cols = slice(hd * A_HEAD_DIM, (hd + 1) * A_HEAD_DIM)
        w_hd = (w_s_ref[hd, 0:chunk, 0:chunk] * tril).astype(_BF16)
        v_hd = jnp.concatenate([v_b[c * chunk:(c + 1) * chunk, cols] for c in range(n_chunks)], axis=1)
        s_hd = jnp.dot(w_hd, v_hd, preferred_element_type=_F32)
        b_hd = bias_ref[0:chunk, cols]
        s_rows = jnp.concatenate(
            [s_hd[:, c * A_HEAD_DIM:(c + 1) * A_HEAD_DIM] + b_hd for c in range(n_chunks)], axis=0)
        y_a_heads.append(u[:, cols] * s_rows)
    y_a = jnp.concatenate(y_a_heads, axis=1)

    first_pos = pos0 + t * rows
    d_parts = []
    for s in range(n_streams):
        d_s, hist = _pool_delta(stream_rows(p, s), pool_s[s], first_pos)
        pool_s[s] = hist
        poolo_ref[s] = hist
        d_parts.append(d_s)
    d = d_parts[0] if n_streams == 1 else jnp.concatenate(d_parts, axis=0)
    y_b = jnp.dot(d.astype(_BF16), w_pool_ref[...], preferred_element_type=_F32) * pscale_ref[...]

    q = gate_c * h_c
    c_parts = []
    for s in range(n_streams):
        c_s, hist = _causal_conv3(stream_rows(q, s), conv_s[s], w_conv_ref[...])
        conv_s[s] = hist
        convo_ref[s] = hist
        c_parts.append(c_s)
    y_c = gate_b * (c_parts[0] if n_streams == 1 else jnp.concatenate(c_parts, axis=0))

    y_mix = jnp.concatenate([y_a, y_b, y_c], axis=1).astype(_BF16)
    x1 = x + jnp.dot(y_mix, w_out_ref[...], preferred_element_type=_F32)

    h2 = _rmsnorm(x1, g2_ref[...]).astype(_BF16)
    n_ff_chunks = D_FF // FF_CHUNK

    def up_proj(c):
        return [jnp.dot(h2, w_up_ref[:, col0:col0 + FF_CHUNK], preferred_element_type=_F32)
                for col0 in (c * FF_CHUNK, D_FF + c * FF_CHUNK)]

    acc = x1
    ups = up_proj(0)
    for c in range(n_ff_chunks):
        ups_next = up_proj(c + 1) if c + 1 < n_ff_chunks else None
        halves = []
        for up, col0 in zip(ups, (c * FF_CHUNK, D_FF + c * FF_CHUNK)):
            tiles = []
            for j in range(col0 // LANES, (col0 + FF_CHUNK) // LANES):
                lanes = slice(j * LANES, (j + 1) * LANES)
                w_fc = w_fconv_ref[:, lanes]
                parts = []
                for s in range(n_streams):
                    up_s = stream_rows(up, s)[:, lanes.start - col0:lanes.stop - col0]
                    parts.append(_causal_conv3_ref(ffn_s.at[s, j], up_s, w_fc))
                    ffno_ref[s, :, lanes] = up_s[rows - CONV_HIST_PAD:]
                conv = parts[0] if n_streams == 1 else jnp.concatenate(parts, axis=0)
                tiles.append(conv + b_fconv_ref[:, lanes])
            halves.append(jnp.concatenate(tiles, axis=1))
        gate, val = halves
        act = (gate * (1.0 / (1.0 + jnp.exp(-gate))) * val).astype(_BF16)
        acc = acc + jnp.dot(act, w_down_ref[c * FF_CHUNK:(c + 1) * FF_CHUNK, :], preferred_element_type=_F32)
        ups = ups_next

    y_ref[...] = _rmsnorm(acc, gf_ref[...]) if final_norm else acc


def _layer_call(layer, x, pool_h, conv_h, ffn_h, params, *, n_groups, n_streams, rows, n_tiles, pos0,
                final_norm, emit_v):
    block_rows = n_streams * rows
    total_streams = n_groups * n_streams

    def rows_spec(width):
        return pl.BlockSpec((block_rows, width), lambda g, t: (g * n_tiles + t, 0))

    def state_spec(pad, width):
        return pl.BlockSpec((n_streams, pad, width), lambda g, t: (g, 0, 0))

    def layer_spec(shape):
        return pl.BlockSpec((None,) + shape, lambda g, t: (layer,) + (0,) * len(shape),
                            pipeline_mode=pl.Buffered(1))

    (g1, w_in, w_s, bias, w_pool, pscale, w_conv, w_out, g2, w_up, w_fconv, b_fconv, w_down, gf) = params
    in_specs = [
        rows_spec(D_MODEL),
        state_spec(POOL_HIST_PAD, D_B), state_spec(CONV_HIST_PAD, D_C), state_spec(CONV_HIST_PAD, 2 * D_FF),
        layer_spec((1, D_MODEL)), layer_spec((D_MODEL, D_IN)),
        layer_spec((N_A_HEADS, GMLP_CHUNK, GMLP_CHUNK)), layer_spec((GMLP_CHUNK, D_A)),
        layer_spec((D_B, D_B)), layer_spec((1, D_B)), layer_spec((3, D_C)),
        layer_spec((D_MODEL, D_MODEL)), layer_spec((1, D_MODEL)),
        layer_spec((D_MODEL, 2 * D_FF)), layer_spec((3, 2 * D_FF)), layer_spec((1, 2 * D_FF)),
        layer_spec((D_FF, D_MODEL)),
        pl.BlockSpec((1, D_MODEL), lambda g, t: (0, 0), pipeline_mode=pl.Buffered(1)),
    ]
    out_shape = [
        jax.ShapeDtypeStruct(x.shape, _F32),
        jax.ShapeDtypeStruct((total_streams, POOL_HIST_PAD, D_B), _F32),
        jax.ShapeDtypeStruct((total_streams, CONV_HIST_PAD, D_C), _F32),
        jax.ShapeDtypeStruct((total_streams, CONV_HIST_PAD, 2 * D_FF), _F32),
    ]
    out_specs = [rows_spec(D_MODEL), state_spec(POOL_HIST_PAD, D_B), state_spec(CONV_HIST_PAD, D_C),
                 state_spec(CONV_HIST_PAD, 2 * D_FF)]
    if emit_v:
        out_shape.append(jax.ShapeDtypeStruct((x.shape[0], D_A), _F32))
        out_specs.append(rows_spec(D_A))
    body = functools.partial(_layer_kernel, n_streams=n_streams, rows=rows, pos0=pos0,
                             final_norm=final_norm, emit_v=emit_v)
    return pl.pallas_call(
        body,
        grid=(n_groups, n_tiles),
        in_specs=in_specs,
        out_specs=out_specs,
        out_shape=out_shape,
        scratch_shapes=[pltpu.VMEM((n_streams, POOL_HIST_PAD, D_B), _F32),
                        pltpu.VMEM((n_streams, CONV_HIST_PAD, D_C), _F32),
                        pltpu.VMEM((n_streams, 2 * D_FF // LANES, CONV_HIST_PAD + rows, LANES), _F32)],
        compiler_params=pltpu.CompilerParams(dimension_semantics=("arbitrary", "arbitrary"),
                                             vmem_limit_bytes=VMEM_LIMIT_BYTES),
        name=("sample" if emit_v else "prompt") + "_layer",
    )(x, pool_h, conv_h, ffn_h, g1, w_in, w_s, bias, w_pool, pscale, w_conv, w_out, g2, w_up, w_fconv,
      b_fconv, w_down, gf)


def _pad_front(a, pad):
    return jnp.pad(a, ((0, 0),) * (a.ndim - 2) + ((pad - a.shape[-2], 0), (0, 0)))


def kernel(x_prompt, x_sample, state_pool, state_conv, state_ffn_conv, norm1_g, w_in, w_s, b_s, w_pool,
           pool_scale, w_conv, w_out, norm2_g, w_up, w_fconv, b_fconv, w_down, final_g):
    depth = w_in.shape[0]
    batch, seq, _ = x_prompt.shape
    dec_batch, dec_seq, _ = x_sample.shape

    group_eye = jnp.eye(len(POOL_WINDOWS), dtype=bool)[None, :, None, :, None]
    w_pool_bd = jnp.where(group_eye, w_pool[:, :, :, None, :], 0.0).reshape(depth, D_B, D_B)
    bias = jnp.repeat(jnp.swapaxes(b_s, 1, 2), A_HEAD_DIM, axis=-1)
    params = (norm1_g[:, None, :], w_in.astype(_BF16), w_s, bias, w_pool_bd.astype(_BF16),
              pool_scale[:, None, :], w_conv, w_out.astype(_BF16), norm2_g[:, None, :], w_up.astype(_BF16),
              w_fconv, b_fconv[:, None, :], w_down.astype(_BF16), final_g[None, :])

    pool_hs = _pad_front(state_pool, POOL_HIST_PAD)
    conv_hs = _pad_front(state_conv, CONV_HIST_PAD)
    ffn_hs = _pad_front(state_ffn_conv, CONV_HIST_PAD)
    zero_pool = jnp.zeros((batch, POOL_HIST_PAD, D_B), _F32)
    zero_conv = jnp.zeros((batch, CONV_HIST_PAD, D_C), _F32)
    zero_ffn = jnp.zeros((batch, CONV_HIST_PAD, 2 * D_FF), _F32)

    xp = x_prompt.reshape(batch * seq, D_MODEL)
    xs = x_sample.reshape(dec_batch * dec_seq, D_MODEL)
    outs_p, outs_s = [], []
    for layer in range(depth):
        last = layer == depth - 1
        xp, *st_p = _layer_call(layer, xp, zero_pool, zero_conv, zero_ffn, params, n_groups=batch, n_streams=1,
                                rows=PROMPT_TILE, n_tiles=seq // PROMPT_TILE, pos0=0, final_norm=last,
                                emit_v=False)
        xs, *st_s = _layer_call(layer, xs, pool_hs[layer], conv_hs[layer], ffn_hs[layer], params, n_groups=1,
                                n_streams=dec_batch, rows=dec_seq, n_tiles=1, pos0=PAST_LEN, final_norm=last,
                                emit_v=True)
        outs_p.append(st_p)
        outs_s.append(st_s)

    def stacked(outs, i, keep):
        return jnp.stack([o[i] for o in outs])[:, :, -keep:, :]

    return (xp.reshape(batch, seq, D_MODEL), xs.reshape(dec_batch, dec_seq, D_MODEL),
            stacked(outs_p, 0, POOL_HIST), stacked(outs_p, 1, CONV_HIST), stacked(outs_p, 2, CONV_HIST),
            stacked(outs_s, 0, POOL_HIST), stacked(outs_s, 1, CONV_HIST), stacked(outs_s, 2, CONV_HIST),
            jnp.stack([o[3] for o in outs_s]).reshape(depth, dec_batch, dec_seq, D_A))
```

```python
import functools

import jax
import jax.numpy as jnp
from jax import lax
from jax.experimental import pallas as pl
from jax.experimental.pallas import tpu as pltpu

D_MODEL = 1024
D_A = 512
N_A_HEADS = 4
A_HEAD_DIM = 128
GMLP_CHUNK = 128
D_B = 256
POOL_WINDOWS = (2, 4, 8, 16)
POOL_GROUP = 64
POOL_HIST = 15
D_C = 256
CONV_HIST = 2
D_IN = 2 * D_A + D_B + 3 * D_C
D_FF = 11 * D_MODEL // 4
EPS = 1e-6
PAST_LEN = 4096

LANES = 128
SUBLANES = 8
POOL_HIST_PAD = 16
CONV_HIST_PAD = SUBLANES
FF_CHUNK = 256
PROMPT_TILE = 256
PROMPT_TILES_PER_STEP = 2
FF_LOOKAHEAD = 2
MIXER_PIECES_AT_CHUNK = {0: 1, 1: 1, 2: 1, 4: 1, 5: 1, 7: 1, 9: 1}
VMEM_LIMIT_BYTES = 56 * 1024 * 1024

_F32 = jnp.float32
_BF16 = jnp.bfloat16


def _rmsnorm(x, g):
    y = x * lax.rsqrt(jnp.mean(x * x, axis=-1, keepdims=True) + EPS)
    return y * g


def _causal_conv3_ref(buf, cur, w):
    rows = cur.shape[0]
    buf[CONV_HIST_PAD:, :] = cur
    out = (w[0:1] * buf[CONV_HIST_PAD - 2:CONV_HIST_PAD - 2 + rows, :]
           + w[1:2] * buf[CONV_HIST_PAD - 1:CONV_HIST_PAD - 1 + rows, :] + w[2:3] * cur)
    buf[0:CONV_HIST_PAD, :] = cur[rows - CONV_HIST_PAD:]
    return out


def _pool_window_sums(buf, hist, p, wide):
    rows = p.shape[0]
    n = POOL_HIST_PAD + rows
    a, b = buf.at[0], buf.at[1]
    a[SUBLANES:SUBLANES + POOL_HIST_PAD, :] = hist
    a[SUBLANES + POOL_HIST_PAD:, :] = p
    s2 = a[SUBLANES:SUBLANES + n, :] + a[SUBLANES - 1:SUBLANES - 1 + n, :]
    b[SUBLANES:SUBLANES + n, :] = s2
    s4 = s2 + b[SUBLANES - 2:SUBLANES - 2 + n, :]
    if not wide:
        return s2[POOL_HIST_PAD:], s4[POOL_HIST_PAD:]
    a[SUBLANES:SUBLANES + n, :] = s4
    s8 = s4 + a[SUBLANES - 4:SUBLANES - 4 + n, :]
    s16 = s8[POOL_HIST_PAD:] + s8[POOL_HIST_PAD - 8:n - 8]
    return s8[POOL_HIST_PAD:], s16


def _pool_delta(buf, hist, p, first_pos):
    rows = p.shape[0]
    low = lax.broadcasted_iota(jnp.int32, (rows, LANES), 1) < POOL_GROUP
    s2, s4 = _pool_window_sums(buf.at[0], hist[:, :LANES], p[:, :LANES], wide=False)
    s8, s16 = _pool_window_sums(buf.at[1], hist[:, LANES:], p[:, LANES:], wide=True)
    sums = jnp.concatenate([jnp.where(low, s2, s4), jnp.where(low, s8, s16)], axis=1)
    lane_b = lax.broadcasted_iota(jnp.int32, (rows, D_B), 1)
    window = jnp.where(lane_b < POOL_GROUP, POOL_WINDOWS[0],
                       jnp.where(lane_b < 2 * POOL_GROUP, POOL_WINDOWS[1],
                                 jnp.where(lane_b < 3 * POOL_GROUP, POOL_WINDOWS[2], POOL_WINDOWS[3])))
    pos1 = lax.broadcasted_iota(jnp.int32, (rows, D_B), 0) + (first_pos + 1)
    cnt = jnp.minimum(pos1, window).astype(_F32)
    return sums / cnt - p


def _layer_kernel(x_ref, poolh_ref, convh_ref, ffnh_ref, g1_ref, w_in_ref, w_s_ref, bias_ref, w_pool_ref,
                  pscale_ref, w_conv_ref, w_out_ref, g2_ref, w_up_ref, w_fconv_ref, b_fconv_ref, w_down_ref,
                  gf_ref, *rest, n_streams, rows, n_sub, steps_per_group, pos0, final_norm, emit_v):
    rest = list(rest)
    y_ref, poolo_ref, convo_ref, ffno_ref = rest[:4]
    v_ref = rest[4] if emit_v else None
    pool_s, pool_buf, conv_buf, ffn_buf = rest[5:] if emit_v else rest[4:]
    tile_rows = n_streams * rows

    seq_step = lax.rem(pl.program_id(0), steps_per_group)

    @pl.when(seq_step == 0)
    def _load_history():
        pool_s[...] = poolh_ref[...]
        for s in range(n_streams):
            for j in range(D_C // LANES):
                conv_buf[s, j, 0:CONV_HIST_PAD, :] = convh_ref[s, :, j * LANES:(j + 1) * LANES]
            for j in range(D_B // LANES):
                for k in range(2):
                    pool_buf[s, j, k, 0:SUBLANES, :] = jnp.zeros((SUBLANES, LANES), _F32)
            for j in range(2 * D_FF // LANES):
                ffn_buf[s, j, 0:CONV_HIST_PAD, :] = ffnh_ref[s, :, j * LANES:(j + 1) * LANES]

    def stream_rows(a, s):
        return a[s * rows:(s + 1) * rows]

    def per_stream(fn):
        parts = [fn(s) for s in range(n_streams)]
        return parts[0] if n_streams == 1 else jnp.concatenate(parts, axis=0)

    def mixer_half(sub, out):
        block = slice(sub * tile_rows, (sub + 1) * tile_rows)
        x = x_ref[block, :]
        h = _rmsnorm(x, g1_ref[...]).astype(_BF16)
        yield
        z_uv = jnp.dot(h, w_in_ref[:, 0:2 * D_A], preferred_element_type=_F32)
        u = z_uv[:, 0:D_A]
        v = z_uv[:, D_A:]
        if emit_v:
            v_ref[block, :] = v
        yield
        z_r = jnp.dot(h, w_in_ref[:, 2 * D_A:], preferred_element_type=_F32)
        p = z_r[:, 0:D_B]
        gate_b = z_r[:, D_B:D_B + D_C]
        gate_c = z_r[:, D_B + D_C:D_B + 2 * D_C]
        h_c = z_r[:, D_B + 2 * D_C:]
        yield

        chunk = min(rows, GMLP_CHUNK)
        n_chunks = tile_rows // chunk
        v_b = v.astype(_BF16)
        tril = (lax.broadcasted_iota(jnp.int32, (chunk, chunk), 0)
                >= lax.broadcasted_iota(jnp.int32, (chunk, chunk), 1)).astype(_F32)
        y_a_heads = []
        for hd in range(N_A_HEADS):
            cols = slice(hd * A_HEAD_DIM, (hd + 1) * A_HEAD_DIM)
            w_hd = (w_s_ref[hd, 0:chunk, 0:chunk] * tril).astype(_BF16)
            v_hd = jnp.concatenate([v_b[c * chunk:(c + 1) * chunk, cols] for c in range(n_chunks)], axis=1)
            s_hd = jnp.dot(w_hd, v_hd, preferred_element_type=_F32)
            b_hd = bias_ref[0:chunk, cols]
            s_rows = jnp.concatenate(
                [s_hd[:, c * A_HEAD_DIM:(c + 1) * A_HEAD_DIM] + b_hd for c in range(n_chunks)], axis=0)
            y_a_heads.append(u[:, cols] * s_rows)
        y_a = jnp.concatenate(y_a_heads, axis=1)
        yield

        first_pos = pos0 + (seq_step * n_sub + sub) * rows

        def pool_stream(s):
            p_s = stream_rows(p, s)
            d_s = _pool_delta(pool_buf.at[s], pool_s[s], p_s, first_pos)
            pool_s[s] = p_s[rows - POOL_HIST_PAD:]
            poolo_ref[s] = p_s[rows - POOL_HIST_PAD:]
            return d_s

        d = per_stream(pool_stream)
        y_b = jnp.dot(d.astype(_BF16), w_pool_ref[...], preferred_element_type=_F32) * pscale_ref[...]
        yield

        q = gate_c * h_c
        conv_tiles = []
        for j in range(D_C // LANES):
            lanes = slice(j * LANES, (j + 1) * LANES)

            def conv_stream(s, j=j, lanes=lanes):
                q_s = stream_rows(q, s)[:, lanes]
                convo_ref[s, :, lanes] = q_s[rows - CONV_HIST_PAD:]
                return _causal_conv3_ref(conv_buf.at[s, j], q_s, w_conv_ref[:, lanes])

            conv_tiles.append(per_stream(conv_stream))
        y_c = gate_b * jnp.concatenate(conv_tiles, axis=1)
        y_mix = jnp.concatenate([y_a, y_b, y_c], axis=1).astype(_BF16)
        x1 = x + jnp.dot(y_mix, w_out_ref[...], preferred_element_type=_F32)
        yield
        out["x1"] = x1
        out["h2"] = _rmsnorm(x1, g2_ref[...]).astype(_BF16)

    def ffn_half(x1, h2, between_chunks):
        n_ff_chunks = D_FF // FF_CHUNK

        def up_proj(c):
            return [jnp.dot(h2, w_up_ref[:, col0:col0 + FF_CHUNK], preferred_element_type=_F32)
                    for col0 in (c * FF_CHUNK, D_FF + c * FF_CHUNK)]

        acc = x1
        ups_queue = [up_proj(c) for c in range(FF_LOOKAHEAD)]
        for c in range(n_ff_chunks):
            between_chunks(c)
            if c + FF_LOOKAHEAD < n_ff_chunks:
                ups_queue.append(up_proj(c + FF_LOOKAHEAD))
            ups = ups_queue.pop(0)
            halves = []
            for up, col0 in zip(ups, (c * FF_CHUNK, D_FF + c * FF_CHUNK)):
                tiles = []
                for j in range(col0 // LANES, (col0 + FF_CHUNK) // LANES):
                    lanes = slice(j * LANES, (j + 1) * LANES)

                    def conv_stream(s, j=j, lanes=lanes, up=up, col0=col0):
                        up_s = stream_rows(up, s)[:, lanes.start - col0:lanes.stop - col0]
                        ffno_ref[s, :, lanes] = up_s[rows - CONV_HIST_PAD:]
                        return _causal_conv3_ref(ffn_buf.at[s, j], up_s, w_fconv_ref[:, lanes])

                    tiles.append(per_stream(conv_stream) + b_fconv_ref[:, lanes])
                halves.append(jnp.concatenate(tiles, axis=1))
            gate, val = halves
            act = (gate * (1.0 / (1.0 + jnp.exp(-gate))) * val).astype(_BF16)
            acc = acc + jnp.dot(act, w_down_ref[c * FF_CHUNK:(c + 1) * FF_CHUNK, :],
                                preferred_element_type=_F32)
        return acc

    outs = [{} for _ in range(n_sub)]
    mixers = [mixer_half(sub, outs[sub]) for sub in range(n_sub)]
    for _ in mixers[0]:
        pass
    for sub in range(n_sub):
        following = mixers[sub + 1] if sub + 1 < n_sub else iter(())

        def advance_following(c, following=following):
            for _ in range(MIXER_PIECES_AT_CHUNK.get(c, 0)):
                next(following, None)

        acc = ffn_half(outs[sub]["x1"], outs[sub]["h2"], advance_following)
        for _ in following:
            pass
        y_ref[sub * tile_rows:(sub + 1) * tile_rows, :] = _rmsnorm(acc, gf_ref[...]) if final_norm else acc


def _layer_call(layer, x, pool_h, conv_h, ffn_h, params, *, n_groups, n_streams, rows, n_sub, steps_per_group,
                pos0, final_norm, emit_v):
    block_rows = n_sub * n_streams * rows
    total_streams = n_groups * n_streams

    def rows_spec(width):
        return pl.BlockSpec((block_rows, width), lambda t: (t, 0))

    def state_spec(pad, width):
        return pl.BlockSpec((n_streams, pad, width), lambda t: (t // steps_per_group, 0, 0))

    def layer_spec(shape):
        return pl.BlockSpec((None,) + shape, lambda t: (layer,) + (0,) * len(shape),
                            pipeline_mode=pl.Buffered(1))

    (g1, w_in, w_s, bias, w_pool, pscale, w_conv, w_out, g2, w_up, w_fconv, b_fconv, w_down, gf) = params
    in_specs = [
        rows_spec(D_MODEL),
        state_spec(POOL_HIST_PAD, D_B), state_spec(CONV_HIST_PAD, D_C), state_spec(CONV_HIST_PAD, 2 * D_FF),
        layer_spec((1, D_MODEL)), layer_spec((D_MODEL, D_IN)),
        layer_spec((N_A_HEADS, GMLP_CHUNK, GMLP_CHUNK)), layer_spec((GMLP_CHUNK, D_A)),
        layer_spec((D_B, D_B)), layer_spec((1, D_B)), layer_spec((3, D_C)),
        layer_spec((D_MODEL, D_MODEL)), layer_spec((1, D_MODEL)),
        layer_spec((D_MODEL, 2 * D_FF)), layer_spec((3, 2 * D_FF)), layer_spec((1, 2 * D_FF)),
        layer_spec((D_FF, D_MODEL)),
        pl.BlockSpec((1, D_MODEL), lambda t: (0, 0), pipeline_mode=pl.Buffered(1)),
    ]
    out_shape = [
        jax.ShapeDtypeStruct(x.shape, _F32),
        jax.ShapeDtypeStruct((total_streams, POOL_HIST_PAD, D_B), _F32),
        jax.ShapeDtypeStruct((total_streams, CONV_HIST_PAD, D_C), _F32),
        jax.ShapeDtypeStruct((total_streams, CONV_HIST_PAD, 2 * D_FF), _F32),
    ]
    out_specs = [rows_spec(D_MODEL), state_spec(POOL_HIST_PAD, D_B), state_spec(CONV_HIST_PAD, D_C),
                 state_spec(CONV_HIST_PAD, 2 * D_FF)]
    if emit_v:
        out_shape.append(jax.ShapeDtypeStruct((x.shape[0], D_A), _F32))
        out_specs.append(rows_spec(D_A))
    scratch_shapes = [
        pltpu.VMEM((n_streams, POOL_HIST_PAD, D_B), _F32),
        pltpu.VMEM((n_streams, D_B // LANES, 2, SUBLANES + POOL_HIST_PAD + rows, LANES), _F32),
        pltpu.VMEM((n_streams, D_C // LANES, CONV_HIST_PAD + rows, LANES), _F32),
        pltpu.VMEM((n_streams, 2 * D_FF // LANES, CONV_HIST_PAD + rows, LANES), _F32),
    ]
    body = functools.partial(_layer_kernel, n_streams=n_streams, rows=rows, n_sub=n_sub,
                             steps_per_group=steps_per_group, pos0=pos0, final_norm=final_norm, emit_v=emit_v)
    return pl.pallas_call(
        body,
        grid=(n_groups * steps_per_group,),
        in_specs=in_specs,
        out_specs=out_specs,
        out_shape=out_shape,
        scratch_shapes=scratch_shapes,
        compiler_params=pltpu.CompilerParams(dimension_semantics=("arbitrary",),
                                             vmem_limit_bytes=VMEM_LIMIT_BYTES),
        name=("sample" if emit_v else "prompt") + "_layer",
    )(x, pool_h, conv_h, ffn_h, g1, w_in, w_s, bias, w_pool, pscale, w_conv, w_out, g2, w_up, w_fconv,
      b_fconv, w_down, gf)


def _pad_front(a, pad):
    return jnp.pad(a, ((0, 0),) * (a.ndim - 2) + ((pad - a.shape[-2], 0), (0, 0)))


def kernel(x_prompt, x_sample, state_pool, state_conv, state_ffn_conv, norm1_g, w_in, w_s, b_s, w_pool,
           pool_scale, w_conv, w_out, norm2_g, w_up, w_fconv, b_fconv, w_down, final_g):
    depth = w_in.shape[0]
    batch, seq, _ = x_prompt.shape
    dec_batch, dec_seq, _ = x_sample.shape

    group_eye = jnp.eye(len(POOL_WINDOWS), dtype=bool)[None, :, None, :, None]
    w_pool_bd = jnp.where(group_eye, w_pool[:, :, :, None, :], 0.0).reshape(depth, D_B, D_B)
    bias = jnp.repeat(jnp.swapaxes(b_s, 1, 2), A_HEAD_DIM, axis=-1)
    params = (norm1_g[:, None, :], w_in.astype(_BF16), w_s, bias, w_pool_bd.astype(_BF16),
              pool_scale[:, None, :], w_conv, w_out.astype(_BF16), norm2_g[:, None, :], w_up.astype(_BF16),
              w_fconv, b_fconv[:, None, :], w_down.astype(_BF16), final_g[None, :])

    pool_hs = _pad_front(state_pool, POOL_HIST_PAD)
    conv_hs = _pad_front(state_conv, CONV_HIST_PAD)
    ffn_hs = _pad_front(state_ffn_conv, CONV_HIST_PAD)
    zero_pool = jnp.zeros((batch, POOL_HIST_PAD, D_B), _F32)
    zero_conv = jnp.zeros((batch, CONV_HIST_PAD, D_C), _F32)
    zero_ffn = jnp.zeros((batch, CONV_HIST_PAD, 2 * D_FF), _F32)

    xp = x_prompt.reshape(batch * seq, D_MODEL)
    xs = x_sample.reshape(dec_batch * dec_seq, D_MODEL)
    outs_p, outs_s = [], []
    for layer in range(depth):
        last = layer == depth - 1
        xp, *st_p = _layer_call(layer, xp, zero_pool, zero_conv, zero_ffn, params, n_groups=batch, n_streams=1,
                                rows=PROMPT_TILE, n_sub=PROMPT_TILES_PER_STEP,
                                steps_per_group=seq // (PROMPT_TILE * PROMPT_TILES_PER_STEP), pos0=0,
                                final_norm=last, emit_v=False)
        xs, *st_s = _layer_call(layer, xs, pool_hs[layer], conv_hs[layer], ffn_hs[layer], params, n_groups=1,
                                n_streams=dec_batch, rows=dec_seq, n_sub=1, steps_per_group=1, pos0=PAST_LEN,
                                final_norm=last, emit_v=True)
        outs_p.append(st_p)
        outs_s.append(st_s)

    def stacked(outs, i, keep):
        return jnp.stack([o[i] for o in outs])[:, :, -keep:, :]

    return (xp.reshape(batch, seq, D_MODEL), xs.reshape(dec_batch, dec_seq, D_MODEL),
            stacked(outs_p, 0, POOL_HIST), stacked(outs_p, 1, CONV_HIST), stacked(outs_p, 2, CONV_HIST),
            stacked(outs_s, 0, POOL_HIST), stacked(outs_s, 1, CONV_HIST), stacked(outs_s, 2, CONV_HIST),
            jnp.stack([o[3] for o in outs_s]).reshape(depth, dec_batch, dec_seq, D_A))
```

```python
import functools

import jax
import jax.numpy as jnp
from jax import lax
from jax.experimental import pallas as pl
from jax.experimental.pallas import tpu as pltpu

D_MODEL = 1024
D_A = 512
N_A_HEADS = 4
A_HEAD_DIM = 128
GMLP_CHUNK = 128
D_B = 256
POOL_WINDOWS = (2, 4, 8, 16)
POOL_GROUP = 64
POOL_HIST = 15
D_C = 256
CONV_HIST = 2
D_IN = 2 * D_A + D_B + 3 * D_C
D_FF = 11 * D_MODEL // 4
EPS = 1e-6
PAST_LEN = 4096

LANES = 128
SUBLANES = 8
POOL_HIST_PAD = 16
CONV_HIST_PAD = SUBLANES
FF_CHUNK = 256
PROMPT_TILE = 256
PROMPT_TILES_PER_STEP = 4
FF_LOOKAHEAD = 2
MIXER_PIECES_AT_CHUNK = {0: 1, 1: 1, 2: 1, 4: 1, 5: 1, 7: 1, 9: 1}
VMEM_LIMIT_BYTES = 56 * 1024 * 1024

_F32 = jnp.float32
_BF16 = jnp.bfloat16


def _rmsnorm(x, g):
    y = x * lax.rsqrt(jnp.mean(x * x, axis=-1, keepdims=True) + EPS)
    return y * g


def _causal_conv3_ref(buf, cur, w):
    rows = cur.shape[0]
    buf[CONV_HIST_PAD:, :] = cur
    out = (w[0:1] * buf[CONV_HIST_PAD - 2:CONV_HIST_PAD - 2 + rows, :]
           + w[1:2] * buf[CONV_HIST_PAD - 1:CONV_HIST_PAD - 1 + rows, :] + w[2:3] * cur)
    buf[0:CONV_HIST_PAD, :] = cur[rows - CONV_HIST_PAD:]
    return out


def _pool_window_sums(buf, hist, p, wide):
    rows = p.shape[0]
    n = POOL_HIST_PAD + rows
    a, b = buf.at[0], buf.at[1]
    a[SUBLANES:SUBLANES + POOL_HIST_PAD, :] = hist
    a[SUBLANES + POOL_HIST_PAD:, :] = p
    s2 = a[SUBLANES:SUBLANES + n, :] + a[SUBLANES - 1:SUBLANES - 1 + n, :]
    b[SUBLANES:SUBLANES + n, :] = s2
    s4 = s2 + b[SUBLANES - 2:SUBLANES - 2 + n, :]
    if not wide:
        return s2[POOL_HIST_PAD:], s4[POOL_HIST_PAD:]
    a[SUBLANES:SUBLANES + n, :] = s4
    s8 = s4 + a[SUBLANES - 4:SUBLANES - 4 + n, :]
    s16 = s8[POOL_HIST_PAD:] + s8[POOL_HIST_PAD - 8:n - 8]
    return s8[POOL_HIST_PAD:], s16


def _pool_delta(buf, hist, p, first_pos):
    rows = p.shape[0]
    low = lax.broadcasted_iota(jnp.int32, (rows, LANES), 1) < POOL_GROUP
    s2, s4 = _pool_window_sums(buf.at[0], hist[:, :LANES], p[:, :LANES], wide=False)
    s8, s16 = _pool_window_sums(buf.at[1], hist[:, LANES:], p[:, LANES:], wide=True)
    sums = jnp.concatenate([jnp.where(low, s2, s4), jnp.where(low, s8, s16)], axis=1)
    lane_b = lax.broadcasted_iota(jnp.int32, (rows, D_B), 1)
    window = jnp.where(lane_b < POOL_GROUP, POOL_WINDOWS[0],
                       jnp.where(lane_b < 2 * POOL_GROUP, POOL_WINDOWS[1],
                                 jnp.where(lane_b < 3 * POOL_GROUP, POOL_WINDOWS[2], POOL_WINDOWS[3])))
    pos1 = lax.broadcasted_iota(jnp.int32, (rows, D_B), 0) + (first_pos + 1)
    cnt = jnp.minimum(pos1, window).astype(_F32)
    return sums / cnt - p


def _layer_kernel(x_ref, poolh_ref, convh_ref, ffnh_ref, g1_ref, w_in_ref, w_s_ref, bias_ref, w_pool_ref,
                  pscale_ref, w_conv_ref, w_out_ref, g2_ref, w_up_ref, w_fconv_ref, b_fconv_ref, w_down_ref,
                  gf_ref, *rest, n_streams, rows, n_sub, steps_per_group, pos0, final_norm, emit_v):
    rest = list(rest)
    y_ref, poolo_ref, convo_ref, ffno_ref = rest[:4]
    v_ref = rest[4] if emit_v else None
    pool_s, pool_buf, conv_buf, ffn_buf = rest[5:] if emit_v else rest[4:]
    tile_rows = n_streams * rows

    seq_step = lax.rem(pl.program_id(0), steps_per_group)

    @pl.when(seq_step == 0)
    def _load_history():
        pool_s[...] = poolh_ref[...]
        for s in range(n_streams):
            for j in range(D_C // LANES):
                conv_buf[s, j, 0:CONV_HIST_PAD, :] = convh_ref[s, :, j * LANES:(j + 1) * LANES]
            for j in range(D_B // LANES):
                for k in range(2):
                    pool_buf[s, j, k, 0:SUBLANES, :] = jnp.zeros((SUBLANES, LANES), _F32)
            for j in range(2 * D_FF // LANES):
                ffn_buf[s, j, 0:CONV_HIST_PAD, :] = ffnh_ref[s, :, j * LANES:(j + 1) * LANES]

    def stream_rows(a, s):
        return a[s * rows:(s + 1) * rows]

    def per_stream(fn):
        parts = [fn(s) for s in range(n_streams)]
        return parts[0] if n_streams == 1 else jnp.concatenate(parts, axis=0)

    def mixer_half(sub, out):
        block = slice(sub * tile_rows, (sub + 1) * tile_rows)
        x = x_ref[block, :]
        h = _rmsnorm(x, g1_ref[...]).astype(_BF16)
        yield
        z_uv = jnp.dot(h, w_in_ref[:, 0:2 * D_A], preferred_element_type=_F32)
        u = z_uv[:, 0:D_A]
        v = z_uv[:, D_A:]
        if emit_v:
            v_ref[block, :] = v
        yield
        z_r = jnp.dot(h, w_in_ref[:, 2 * D_A:], preferred_element_type=_F32)
        p = z_r[:, 0:D_B]
        gate_b = z_r[:, D_B:D_B + D_C]
        gate_c = z_r[:, D_B + D_C:D_B + 2 * D_C]
        h_c = z_r[:, D_B + 2 * D_C:]
        yield

        chunk = min(rows, GMLP_CHUNK)
        n_chunks = tile_rows // chunk
        v_b = v.astype(_BF16)
        tril = (lax.broadcasted_iota(jnp.int32, (chunk, chunk), 0)
                >= lax.broadcasted_iota(jnp.int32, (chunk, chunk), 1)).astype(_F32)
        y_a_heads = []
        for hd in range(N_A_HEADS):
            cols = slice(hd * A_HEAD_DIM, (hd + 1) * A_HEAD_DIM)
            w_hd = (w_s_ref[hd, 0:chunk, 0:chunk] * tril).astype(_BF16)
            v_hd = jnp.concatenate([v_b[c * chunk:(c + 1) * chunk, cols] for c in range(n_chunks)], axis=1)
            s_hd = jnp.dot(w_hd, v_hd, preferred_element_type=_F32)
            b_hd = bias_ref[0:chunk, cols]
            s_rows = jnp.concatenate(
                [s_hd[:, c * A_HEAD_DIM:(c + 1) * A_HEAD_DIM] + b_hd for c in range(n_chunks)], axis=0)
            y_a_heads.append(u[:, cols] * s_rows)
        y_a = jnp.concatenate(y_a_heads, axis=1)
        yield

        first_pos = pos0 + (seq_step * n_sub + sub) * rows

        def pool_stream(s):
            p_s = stream_rows(p, s)
            d_s = _pool_delta(pool_buf.at[s], pool_s[s], p_s, first_pos)
            pool_s[s] = p_s[rows - POOL_HIST_PAD:]
            poolo_ref[s] = p_s[rows - POOL_HIST_PAD:]
            return d_s

        d = per_stream(pool_stream)
        y_b = jnp.dot(d.astype(_BF16), w_pool_ref[...], preferred_element_type=_F32) * pscale_ref[...]
        yield

        q = gate_c * h_c
        conv_tiles = []
        for j in range(D_C // LANES):
            lanes = slice(j * LANES, (j + 1) * LANES)

            def conv_stream(s, j=j, lanes=lanes):
                q_s = stream_rows(q, s)[:, lanes]
                convo_ref[s, :, lanes] = q_s[rows - CONV_HIST_PAD:]
                return _causal_conv3_ref(conv_buf.at[s, j], q_s, w_conv_ref[:, lanes])

            conv_tiles.append(per_stream(conv_stream))
        y_c = gate_b * jnp.concatenate(conv_tiles, axis=1)
        y_mix = jnp.concatenate([y_a, y_b, y_c], axis=1).astype(_BF16)
        x1 = x + jnp.dot(y_mix, w_out_ref[...], preferred_element_type=_F32)
        yield
        out["x1"] = x1
        out["h2"] = _rmsnorm(x1, g2_ref[...]).astype(_BF16)

    def ffn_half(x1, h2, between_chunks):
        n_ff_chunks = D_FF // FF_CHUNK

        def up_proj(c):
            return [jnp.dot(h2, w_up_ref[:, col0:col0 + FF_CHUNK], preferred_element_type=_F32)
                    for col0 in (c * FF_CHUNK, D_FF + c * FF_CHUNK)]

        acc = x1
        ups_queue = [up_proj(c) for c in range(FF_LOOKAHEAD)]
        for c in range(n_ff_chunks):
            between_chunks(c)
            if c + FF_LOOKAHEAD < n_ff_chunks:
                ups_queue.append(up_proj(c + FF_LOOKAHEAD))
            ups = ups_queue.pop(0)
            halves = []
            for up, col0 in zip(ups, (c * FF_CHUNK, D_FF + c * FF_CHUNK)):
                tiles = []
                for j in range(col0 // LANES, (col0 + FF_CHUNK) // LANES):
                    lanes = slice(j * LANES, (j + 1) * LANES)

                    def conv_stream(s, j=j, lanes=lanes, up=up, col0=col0):
                        up_s = stream_rows(up, s)[:, lanes.start - col0:lanes.stop - col0]
                        ffno_ref[s, :, lanes] = up_s[rows - CONV_HIST_PAD:]
                        return _causal_conv3_ref(ffn_buf.at[s, j], up_s, w_fconv_ref[:, lanes])

                    tiles.append(per_stream(conv_stream) + b_fconv_ref[:, lanes])
                halves.append(jnp.concatenate(tiles, axis=1))
            gate, val = halves
            act = (gate * (1.0 / (1.0 + jnp.exp(-gate))) * val).astype(_BF16)
            acc = acc + jnp.dot(act, w_down_ref[c * FF_CHUNK:(c + 1) * FF_CHUNK, :],
                                preferred_element_type=_F32)
        return acc

    outs = [{} for _ in range(n_sub)]
    mixers = [mixer_half(sub, outs[sub]) for sub in range(n_sub)]
    for _ in mixers[0]:
        pass
    for sub in range(n_sub):
        following = mixers[sub + 1] if sub + 1 < n_sub else iter(())

        def advance_following(c, following=following):
            for _ in range(MIXER_PIECES_AT_CHUNK.get(c, 0)):
                next(following, None)

        acc = ffn_half(outs[sub]["x1"], outs[sub]["h2"], advance_following)
        for _ in following:
            pass
        y_ref[sub * tile_rows:(sub + 1) * tile_rows, :] = _rmsnorm(acc, gf_ref[...]) if final_norm else acc


def _layer_call(layer, x, pool_h, conv_h, ffn_h, params, *, n_groups, n_streams, rows, n_sub, steps_per_group,
                pos0, final_norm, emit_v):
    block_rows = n_sub * n_streams * rows
    total_streams = n_groups * n_streams

    def rows_spec(width):
        return pl.BlockSpec((block_rows, width), lambda t: (t, 0))

    def state_spec(pad, width):
        return pl.BlockSpec((n_streams, pad, width), lambda t: (t // steps_per_group, 0, 0))

    def layer_spec(shape):
        return pl.BlockSpec((None,) + shape, lambda t: (layer,) + (0,) * len(shape),
                            pipeline_mode=pl.Buffered(1))

    (g1, w_in, w_s, bias, w_pool, pscale, w_conv, w_out, g2, w_up, w_fconv, b_fconv, w_down, gf) = params
    in_specs = [
        rows_spec(D_MODEL),
        state_spec(POOL_HIST_PAD, D_B), state_spec(CONV_HIST_PAD, D_C), state_spec(CONV_HIST_PAD, 2 * D_FF),
        layer_spec((1, D_MODEL)), layer_spec((D_MODEL, D_IN)),
        layer_spec((N_A_HEADS, GMLP_CHUNK, GMLP_CHUNK)), layer_spec((GMLP_CHUNK, D_A)),
        layer_spec((D_B, D_B)), layer_spec((1, D_B)), layer_spec((3, D_C)),
        layer_spec((D_MODEL, D_MODEL)), layer_spec((1, D_MODEL)),
        layer_spec((D_MODEL, 2 * D_FF)), layer_spec((3, 2 * D_FF)), layer_spec((1, 2 * D_FF)),
        layer_spec((D_FF, D_MODEL)),
        pl.BlockSpec((1, D_MODEL), lambda t: (0, 0), pipeline_mode=pl.Buffered(1)),
    ]
    out_shape = [
        jax.ShapeDtypeStruct(x.shape, _F32),
        jax.ShapeDtypeStruct((total_streams, POOL_HIST_PAD, D_B), _F32),
        jax.ShapeDtypeStruct((total_streams, CONV_HIST_PAD, D_C), _F32),
        jax.ShapeDtypeStruct((total_streams, CONV_HIST_PAD, 2 * D_FF), _F32),
    ]
    out_specs = [rows_spec(D_MODEL), state_spec(POOL_HIST_PAD, D_B), state_spec(CONV_HIST_PAD, D_C),
                 state_spec(CONV_HIST_PAD, 2 * D_FF)]
    if emit_v:
        out_shape.append(jax.ShapeDtypeStruct((x.shape[0], D_A), _F32))
        out_specs.append(rows_spec(D_A))
    scratch_shapes = [
        pltpu.VMEM((n_streams, POOL_HIST_PAD, D_B), _F32),
        pltpu.VMEM((n_streams, D_B // LANES, 2, SUBLANES + POOL_HIST_PAD + rows, LANES), _F32),
        pltpu.VMEM((n_streams, D_C // LANES, CONV_HIST_PAD + rows, LANES), _F32),
        pltpu.VMEM((n_streams, 2 * D_FF // LANES, CONV_HIST_PAD + rows, LANES), _F32),
    ]
    body = functools.partial(_layer_kernel, n_streams=n_streams, rows=rows, n_sub=n_sub,
                             steps_per_group=steps_per_group, pos0=pos0, final_norm=final_norm, emit_v=emit_v)
    return pl.pallas_call(
        body,
        grid=(n_groups * steps_per_group,),
        in_specs=in_specs,
        out_specs=out_specs,
        out_shape=out_shape,
        scratch_shapes=scratch_shapes,
        compiler_params=pltpu.CompilerParams(dimension_semantics=("arbitrary",),
                                             vmem_limit_bytes=VMEM_LIMIT_BYTES),
        name=("sample" if emit_v else "prompt") + "_layer",
    )(x, pool_h, conv_h, ffn_h, g1, w_in, w_s, bias, w_pool, pscale, w_conv, w_out, g2, w_up, w_fconv,
      b_fconv, w_down, gf)


def _pad_front(a, pad):
    return jnp.pad(a, ((0, 0),) * (a.ndim - 2) + ((pad - a.shape[-2], 0), (0, 0)))


def kernel(x_prompt, x_sample, state_pool, state_conv, state_ffn_conv, norm1_g, w_in, w_s, b_s, w_pool,
           pool_scale, w_conv, w_out, norm2_g, w_up, w_fconv, b_fconv, w_down, final_g):
    depth = w_in.shape[0]
    batch, seq, _ = x_prompt.shape
    dec_batch, dec_seq, _ = x_sample.shape

    group_eye = jnp.eye(len(POOL_WINDOWS), dtype=bool)[None, :, None, :, None]
    w_pool_bd = jnp.where(group_eye, w_pool[:, :, :, None, :], 0.0).reshape(depth, D_B, D_B)
    bias = jnp.repeat(jnp.swapaxes(b_s, 1, 2), A_HEAD_DIM, axis=-1)
    params = (norm1_g[:, None, :], w_in.astype(_BF16), w_s, bias, w_pool_bd.astype(_BF16),
              pool_scale[:, None, :], w_conv, w_out.astype(_BF16), norm2_g[:, None, :], w_up.astype(_BF16),
              w_fconv, b_fconv[:, None, :], w_down.astype(_BF16), final_g[None, :])

    pool_hs = _pad_front(state_pool, POOL_HIST_PAD)
    conv_hs = _pad_front(state_conv, CONV_HIST_PAD)
    ffn_hs = _pad_front(state_ffn_conv, CONV_HIST_PAD)
    zero_pool = jnp.zeros((batch, POOL_HIST_PAD, D_B), _F32)
    zero_conv = jnp.zeros((batch, CONV_HIST_PAD, D_C), _F32)
    zero_ffn = jnp.zeros((batch, CONV_HIST_PAD, 2 * D_FF), _F32)

    xp = x_prompt.reshape(batch * seq, D_MODEL)
    xs = x_sample.reshape(dec_batch * dec_seq, D_MODEL)
    outs_p, outs_s = [], []
    for layer in range(depth):
        last = layer == depth - 1
        xp, *st_p = _layer_call(layer, xp, zero_pool, zero_conv, zero_ffn, params, n_groups=batch, n_streams=1,
                                rows=PROMPT_TILE, n_sub=PROMPT_TILES_PER_STEP,
                                steps_per_group=seq // (PROMPT_TILE * PROMPT_TILES_PER_STEP), pos0=0,
                                final_norm=last, emit_v=False)
        xs, *st_s = _layer_call(layer, xs, pool_hs[layer], conv_hs[layer], ffn_hs[layer], params, n_groups=1,
                                n_streams=dec_batch, rows=dec_seq, n_sub=1, steps_per_group=1, pos0=PAST_LEN,
                                final_norm=last, emit_v=True)
        outs_p.append(st_p)
        outs_s.append(st_s)

    def stacked(outs, i, keep):
        return jnp.stack([o[i] for o in outs])[:, :, -keep:, :]

    return (xp.reshape(batch, seq, D_MODEL), xs.reshape(dec_batch, dec_seq, D_MODEL),
            stacked(outs_p, 0, POOL_HIST), stacked(outs_p, 1, CONV_HIST), stacked(outs_p, 2, CONV_HIST),
            stacked(outs_s, 0, POOL_HIST), stacked(outs_s, 1, CONV_HIST), stacked(outs_s, 2, CONV_HIST),
            jnp.stack([o[3] for o in outs_s]).reshape(depth, dec_batch, dec_seq, D_A))
```

```python
import functools
import types

import jax
import jax.numpy as jnp
from jax import lax
from jax.experimental import pallas as pl
from jax.experimental.pallas import tpu as pltpu

D_MODEL = 1024
D_A = 512
N_A_HEADS = 4
A_HEAD_DIM = 128
GMLP_CHUNK = 128
D_B = 256
POOL_WINDOWS = (2, 4, 8, 16)
POOL_GROUP = 64
POOL_HIST = 15
D_C = 256
CONV_HIST = 2
D_IN = 2 * D_A + D_B + 3 * D_C
D_FF = 11 * D_MODEL // 4
EPS = 1e-6
PAST_LEN = 4096

LANES = 128
SUBLANES = 8
POOL_HIST_PAD = 16
CONV_HIST_PAD = SUBLANES
FF_CHUNK = 256
N_FF_CHUNKS = D_FF // FF_CHUNK
PROMPT_TILE = 256
PROMPT_TILES_PER_STEP = 2
FF_LOOKAHEAD = 2
MIXER_PIECES_AT_CHUNK = {0: 1, 1: 1, 2: 1, 4: 1, 5: 1, 7: 1, 9: 1}
VMEM_LIMIT_BYTES = 56 * 1024 * 1024

_F32 = jnp.float32
_BF16 = jnp.bfloat16


def _rmsnorm(x, g):
    y = x * lax.rsqrt(jnp.mean(x * x, axis=-1, keepdims=True) + EPS)
    return y * g


def _silu_gate(gate, val):
    return gate * (1.0 / (1.0 + jnp.exp(-gate))) * val


def _causal_conv3_ref(buf, cur, w):
    rows = cur.shape[0]
    buf[CONV_HIST_PAD:, :] = cur
    out = (w[0:1] * buf[CONV_HIST_PAD - 2:CONV_HIST_PAD - 2 + rows, :]
           + w[1:2] * buf[CONV_HIST_PAD - 1:CONV_HIST_PAD - 1 + rows, :] + w[2:3] * cur)
    buf[0:CONV_HIST_PAD, :] = cur[rows - CONV_HIST_PAD:]
    return out


def _pool_window_sums(buf, hist, p, wide):
    rows = p.shape[0]
    n = POOL_HIST_PAD + rows
    a, b = buf.at[0], buf.at[1]
    a[SUBLANES:SUBLANES + POOL_HIST_PAD, :] = hist
    a[SUBLANES + POOL_HIST_PAD:, :] = p
    s2 = a[SUBLANES:SUBLANES + n, :] + a[SUBLANES - 1:SUBLANES - 1 + n, :]
    b[SUBLANES:SUBLANES + n, :] = s2
    s4 = s2 + b[SUBLANES - 2:SUBLANES - 2 + n, :]
    if not wide:
        return s2[POOL_HIST_PAD:], s4[POOL_HIST_PAD:]
    a[SUBLANES:SUBLANES + n, :] = s4
    s8 = s4 + a[SUBLANES - 4:SUBLANES - 4 + n, :]
    s16 = s8[POOL_HIST_PAD:] + s8[POOL_HIST_PAD - 8:n - 8]
    return s8[POOL_HIST_PAD:], s16


def _pool_delta(buf, hist, p, first_pos):
    rows = p.shape[0]
    low = lax.broadcasted_iota(jnp.int32, (rows, LANES), 1) < POOL_GROUP
    s2, s4 = _pool_window_sums(buf.at[0], hist[:, :LANES], p[:, :LANES], wide=False)
    s8, s16 = _pool_window_sums(buf.at[1], hist[:, LANES:], p[:, LANES:], wide=True)
    sums = jnp.concatenate([jnp.where(low, s2, s4), jnp.where(low, s8, s16)], axis=1)
    lane_b = lax.broadcasted_iota(jnp.int32, (rows, D_B), 1)
    window = jnp.where(lane_b < POOL_GROUP, POOL_WINDOWS[0],
                       jnp.where(lane_b < 2 * POOL_GROUP, POOL_WINDOWS[1],
                                 jnp.where(lane_b < 3 * POOL_GROUP, POOL_WINDOWS[2], POOL_WINDOWS[3])))
    pos1 = lax.broadcasted_iota(jnp.int32, (rows, D_B), 0) + (first_pos + 1)
    cnt = jnp.minimum(pos1, window).astype(_F32)
    return sums / cnt - p


def _init_mixer_scratch(m, n_streams):
    for s in range(n_streams):
        for j in range(D_C // LANES):
            m.conv_buf[s, j, 0:CONV_HIST_PAD, :] = m.convh_ref[s, :, j * LANES:(j + 1) * LANES]
        for j in range(D_B // LANES):
            for k in range(2):
                m.pool_buf[s, j, k, 0:SUBLANES, :] = jnp.zeros((SUBLANES, LANES), _F32)


def _mixer_half(x, m, out, *, n_streams, rows, first_pos):
    tile_rows = n_streams * rows

    def stream_rows(a, s):
        return a[s * rows:(s + 1) * rows]

    def per_stream(fn):
        parts = [fn(s) for s in range(n_streams)]
        return parts[0] if n_streams == 1 else jnp.concatenate(parts, axis=0)

    h = _rmsnorm(x, m.g1_ref[...]).astype(_BF16)
    yield
    z_uv = jnp.dot(h, m.w_in(0, 2 * D_A), preferred_element_type=_F32)
    u = z_uv[:, 0:D_A]
    v = z_uv[:, D_A:]
    if m.store_v is not None:
        m.store_v(v)
    yield
    z_r = jnp.dot(h, m.w_in(2 * D_A, D_IN), preferred_element_type=_F32)
    p = z_r[:, 0:D_B]
    gate_b = z_r[:, D_B:D_B + D_C]
    gate_c = z_r[:, D_B + D_C:D_B + 2 * D_C]
    h_c = z_r[:, D_B + 2 * D_C:]
    yield

    chunk = min(rows, GMLP_CHUNK)
    n_chunks = tile_rows // chunk
    v_b = v.astype(_BF16)
    tril = (lax.broadcasted_iota(jnp.int32, (chunk, chunk), 0)
            >= lax.broadcasted_iota(jnp.int32, (chunk, chunk), 1)).astype(_F32)
    y_a_heads = []
    for hd in range(N_A_HEADS):
        cols = slice(hd * A_HEAD_DIM, (hd + 1) * A_HEAD_DIM)
        w_hd = (m.w_s_ref[hd, 0:chunk, 0:chunk] * tril).astype(_BF16)
        v_hd = jnp.concatenate([v_b[c * chunk:(c + 1) * chunk, cols] for c in range(n_chunks)], axis=1)
        s_hd = jnp.dot(w_hd, v_hd, preferred_element_type=_F32)
        b_hd = m.bias_ref[0:chunk, cols]
        s_rows = jnp.concatenate(
            [s_hd[:, c * A_HEAD_DIM:(c + 1) * A_HEAD_DIM] + b_hd for c in range(n_chunks)], axis=0)
        y_a_heads.append(u[:, cols] * s_rows)
    y_a = jnp.concatenate(y_a_heads, axis=1)
    yield

    def pool_stream(s):
        p_s = stream_rows(p, s)
        d_s = _pool_delta(m.pool_buf.at[s], m.pool_hist(s), p_s, first_pos)
        m.store_pool_hist(s, p_s[rows - POOL_HIST_PAD:])
        return d_s

    d = per_stream(pool_stream)
    y_b = jnp.dot(d.astype(_BF16), m.w_pool_ref[...], preferred_element_type=_F32) * m.pscale_ref[...]
    yield

    q = gate_c * h_c
    conv_tiles = []
    for j in range(D_C // LANES):
        lanes = slice(j * LANES, (j + 1) * LANES)

        def conv_stream(s, j=j, lanes=lanes):
            q_s = stream_rows(q, s)[:, lanes]
            m.convo_ref[s, :, lanes] = q_s[rows - CONV_HIST_PAD:]
            return _causal_conv3_ref(m.conv_buf.at[s, j], q_s, m.w_conv_ref[:, lanes])

        conv_tiles.append(per_stream(conv_stream))
    y_c = gate_b * jnp.concatenate(conv_tiles, axis=1)
    y_mix = jnp.concatenate([y_a, y_b, y_c], axis=1).astype(_BF16)
    x1 = x + jnp.dot(y_mix, m.w_out(), preferred_element_type=_F32)
    yield
    out["x1"] = x1
    out["h2"] = _rmsnorm(x1, m.g2_ref[...]).astype(_BF16)


def _prompt_kernel(x_ref, poolh_ref, convh_ref, ffnh_ref, g1_ref, w_in_ref, w_s_ref, bias_ref, w_pool_ref,
                   pscale_ref, w_conv_ref, w_out_ref, g2_ref, w_upg_ref, w_upv_ref, w_fconv_ref, b_fconv_ref,
                   w_down_ref, gf_ref, y_ref, poolo_ref, convo_ref, ffno_ref, pool_s, pool_buf, conv_buf,
                   ffn_buf, *, rows, n_sub, steps_per_group, final_norm):
    seq_step = lax.rem(pl.program_id(0), steps_per_group)

    def store_pool_hist(s, hist):
        pool_s[s] = hist
        poolo_ref[s] = hist

    m = types.SimpleNamespace(
        g1_ref=g1_ref, w_in=lambda lo, hi: w_in_ref[:, lo:hi], w_s_ref=w_s_ref, bias_ref=bias_ref,
        w_pool_ref=w_pool_ref, pscale_ref=pscale_ref, w_conv_ref=w_conv_ref, w_out=lambda: w_out_ref[...],
        g2_ref=g2_ref, store_v=None, pool_buf=pool_buf, conv_buf=conv_buf, convh_ref=convh_ref,
        convo_ref=convo_ref, pool_hist=lambda s: pool_s[s], store_pool_hist=store_pool_hist)

    @pl.when(seq_step == 0)
    def _load_history():
        pool_s[...] = poolh_ref[...]
        _init_mixer_scratch(m, 1)
        for j in range(2 * D_FF // LANES):
            ffn_buf[0, j, 0:CONV_HIST_PAD, :] = ffnh_ref[0, :, j * LANES:(j + 1) * LANES]

    def ffn_half(x1, h2, between_chunks):
        def up_proj(c):
            cols = slice(c * FF_CHUNK, (c + 1) * FF_CHUNK)
            return [jnp.dot(h2, w_ref[:, cols], preferred_element_type=_F32) for w_ref in (w_upg_ref, w_upv_ref)]

        acc = x1
        ups_queue = [up_proj(c) for c in range(FF_LOOKAHEAD)]
        for c in range(N_FF_CHUNKS):
            between_chunks(c)
            if c + FF_LOOKAHEAD < N_FF_CHUNKS:
                ups_queue.append(up_proj(c + FF_LOOKAHEAD))
            ups = ups_queue.pop(0)
            halves = []
            for up, col0 in zip(ups, (c * FF_CHUNK, D_FF + c * FF_CHUNK)):
                tiles = []
                for j in range(col0 // LANES, (col0 + FF_CHUNK) // LANES):
                    lanes = slice(j * LANES, (j + 1) * LANES)
                    up_j = up[:, lanes.start - col0:lanes.stop - col0]
                    ffno_ref[0, :, lanes] = up_j[rows - CONV_HIST_PAD:]
                    tiles.append(_causal_conv3_ref(ffn_buf.at[0, j], up_j, w_fconv_ref[:, lanes])
                                 + b_fconv_ref[:, lanes])
                halves.append(jnp.concatenate(tiles, axis=1))
            act = _silu_gate(*halves).astype(_BF16)
            acc = acc + jnp.dot(act, w_down_ref[c * FF_CHUNK:(c + 1) * FF_CHUNK, :],
                                preferred_element_type=_F32)
        return acc

    outs = [{} for _ in range(n_sub)]
    mixers = [_mixer_half(x_ref[sub * rows:(sub + 1) * rows, :], m, outs[sub], n_streams=1, rows=rows,
                          first_pos=(seq_step * n_sub + sub) * rows) for sub in range(n_sub)]
    for _ in mixers[0]:
        pass
    for sub in range(n_sub):
        following = mixers[sub + 1] if sub + 1 < n_sub else iter(())

        def advance_following(c, following=following):
            for _ in range(MIXER_PIECES_AT_CHUNK.get(c, 0)):
                next(following, None)

        acc = ffn_half(outs[sub]["x1"], outs[sub]["h2"], advance_following)
        for _ in following:
            pass
        y_ref[sub * rows:(sub + 1) * rows, :] = _rmsnorm(acc, gf_ref[...]) if final_norm else acc


def _sample_kernel(x_ref, poolh_ref, convh_ref, ffnh_g_ref, ffnh_v_ref, g1_ref, w_in_ref, w_s_ref, bias_ref,
                   w_pool_ref, pscale_ref, w_conv_ref, w_out_ref, g2_ref, w_upg_ref, w_upv_ref, w_fcg_ref,
                   w_fcv_ref, b_fcg_ref, b_fcv_ref, w_down_ref, gf_ref,
                   y_ref, poolo_ref, convo_ref, ffno_g_ref, ffno_v_ref, v_ref, w_in_b_ref, w_out_b_ref,
                   w_upg_b_ref, w_upv_b_ref, w_down_b_ref,
                   pool_buf, conv_buf, fconv_buf, acc_s, h2_s, *, n_streams, rows, final_norm):
    c = pl.program_id(0)

    @pl.when(c == 0)
    def _mixers():
        w_in_b = w_in_ref[...].astype(_BF16)
        w_out_b = w_out_ref[...].astype(_BF16)
        w_in_b_ref[...] = w_in_b
        w_out_b_ref[...] = w_out_b

        def store_v(v):
            v_ref[...] = v

        def store_pool_hist(s, hist):
            poolo_ref[s] = hist

        m = types.SimpleNamespace(
            g1_ref=g1_ref, w_in=lambda lo, hi: w_in_b[:, lo:hi], w_s_ref=w_s_ref, bias_ref=bias_ref,
            w_pool_ref=w_pool_ref, pscale_ref=pscale_ref, w_conv_ref=w_conv_ref, w_out=lambda: w_out_b,
            g2_ref=g2_ref, store_v=store_v, pool_buf=pool_buf, conv_buf=conv_buf, convh_ref=convh_ref,
            convo_ref=convo_ref, pool_hist=lambda s: poolh_ref[s], store_pool_hist=store_pool_hist)
        _init_mixer_scratch(m, n_streams)
        out = {}
        for _ in _mixer_half(x_ref[...], m, out, n_streams=n_streams, rows=rows, first_pos=PAST_LEN):
            pass
        acc_s[...] = out["x1"]
        h2_s[...] = out["h2"]

    w_g = w_upg_ref[...].astype(_BF16)
    w_v = w_upv_ref[...].astype(_BF16)
    w_d = w_down_ref[...].astype(_BF16)
    w_upg_b_ref[...] = w_g
    w_upv_b_ref[...] = w_v
    w_down_b_ref[...] = w_d

    h2 = h2_s[...]
    halves = []
    for half, (w, hist_ref, w_fc_ref, b_fc_ref, state_ref) in enumerate((
            (w_g, ffnh_g_ref, w_fcg_ref, b_fcg_ref, ffno_g_ref),
            (w_v, ffnh_v_ref, w_fcv_ref, b_fcv_ref, ffno_v_ref))):
        up = jnp.dot(h2, w, preferred_element_type=_F32)
        tiles = []
        for j in range(FF_CHUNK // LANES):
            lanes = slice(j * LANES, (j + 1) * LANES)
            parts = []
            for s in range(n_streams):
                buf = fconv_buf.at[s, half, j]
                buf[0:CONV_HIST_PAD, :] = hist_ref[s, :, lanes]
                up_s = up[s * rows:(s + 1) * rows, lanes]
                state_ref[s, :, lanes] = up_s[rows - CONV_HIST_PAD:]
                parts.append(_causal_conv3_ref(buf, up_s, w_fc_ref[:, lanes]))
            tiles.append(jnp.concatenate(parts, axis=0) + b_fc_ref[:, lanes])
        halves.append(jnp.concatenate(tiles, axis=1))
    act = _silu_gate(*halves).astype(_BF16)
    acc_s[...] += jnp.dot(act, w_d, preferred_element_type=_F32)

    @pl.when(c == N_FF_CHUNKS - 1)
    def _finish():
        acc = acc_s[...]
        y_ref[...] = _rmsnorm(acc, gf_ref[...]) if final_norm else acc


def _compiler_params():
    return pltpu.CompilerParams(dimension_semantics=("arbitrary",), vmem_limit_bytes=VMEM_LIMIT_BYTES)


def _sample_call(layer, x, pool_h, conv_h, ffn_h, p, *, n_streams, rows, final_norm):
    n_rows = n_streams * rows

    def whole(shape):
        return pl.BlockSpec(shape, lambda c: (0,) * len(shape))

    def layer_whole(shape):
        return pl.BlockSpec((None,) + shape, lambda c: (layer,) + (0,) * len(shape), pipeline_mode=pl.Buffered(1))

    def layer_chunk(shape, axis, offset):
        def index(c):
            idx = [0] * len(shape)
            idx[axis] = c + offset
            return (layer,) + tuple(idx)
        return pl.BlockSpec((None,) + shape, index)

    def chunk(shape, axis, offset=0):
        def index(c):
            idx = [0] * len(shape)
            idx[axis] = c + offset
            return tuple(idx)
        return pl.BlockSpec(shape, index)

    hist_chunk = (n_streams, CONV_HIST_PAD, FF_CHUNK)
    in_specs = [
        whole((n_rows, D_MODEL)),
        whole((n_streams, POOL_HIST_PAD, D_B)), whole((n_streams, CONV_HIST_PAD, D_C)),
        chunk(hist_chunk, 2), chunk(hist_chunk, 2, N_FF_CHUNKS),
        layer_whole((1, D_MODEL)), layer_whole((D_MODEL, D_IN)),
        layer_whole((N_A_HEADS, GMLP_CHUNK, GMLP_CHUNK)), layer_whole((GMLP_CHUNK, D_A)),
        layer_whole((D_B, D_B)), layer_whole((1, D_B)), layer_whole((3, D_C)),
        layer_whole((D_MODEL, D_MODEL)), layer_whole((1, D_MODEL)),
        layer_chunk((D_MODEL, FF_CHUNK), 1, 0), layer_chunk((D_MODEL, FF_CHUNK), 1, N_FF_CHUNKS),
        layer_chunk((3, FF_CHUNK), 1, 0), layer_chunk((3, FF_CHUNK), 1, N_FF_CHUNKS),
        layer_chunk((1, FF_CHUNK), 1, 0), layer_chunk((1, FF_CHUNK), 1, N_FF_CHUNKS),
        layer_chunk((FF_CHUNK, D_MODEL), 0, 0),
        whole((1, D_MODEL)),
    ]
    out_shape = [
        jax.ShapeDtypeStruct((n_rows, D_MODEL), _F32),
        jax.ShapeDtypeStruct((n_streams, POOL_HIST_PAD, D_B), _F32),
        jax.ShapeDtypeStruct((n_streams, CONV_HIST_PAD, D_C), _F32),
        jax.ShapeDtypeStruct((n_streams, CONV_HIST_PAD, D_FF), _F32),
        jax.ShapeDtypeStruct((n_streams, CONV_HIST_PAD, D_FF), _F32),
        jax.ShapeDtypeStruct((n_rows, D_A), _F32),
        jax.ShapeDtypeStruct((D_MODEL, D_IN), _BF16),
        jax.ShapeDtypeStruct((D_MODEL, D_MODEL), _BF16),
        jax.ShapeDtypeStruct((D_MODEL, D_FF), _BF16),
        jax.ShapeDtypeStruct((D_MODEL, D_FF), _BF16),
        jax.ShapeDtypeStruct((D_FF, D_MODEL), _BF16),
    ]
    out_specs = [
        whole((n_rows, D_MODEL)),
        whole((n_streams, POOL_HIST_PAD, D_B)), whole((n_streams, CONV_HIST_PAD, D_C)),
        chunk(hist_chunk, 2), chunk(hist_chunk, 2),
        whole((n_rows, D_A)),
        whole((D_MODEL, D_IN)), whole((D_MODEL, D_MODEL)),
        chunk((D_MODEL, FF_CHUNK), 1), chunk((D_MODEL, FF_CHUNK), 1), chunk((FF_CHUNK, D_MODEL), 0),
    ]
    scratch_shapes = [
        pltpu.VMEM((n_streams, D_B // LANES, 2, SUBLANES + POOL_HIST_PAD + rows, LANES), _F32),
        pltpu.VMEM((n_streams, D_C // LANES, CONV_HIST_PAD + rows, LANES), _F32),
        pltpu.VMEM((n_streams, 2, FF_CHUNK // LANES, CONV_HIST_PAD + rows, LANES), _F32),
        pltpu.VMEM((n_rows, D_MODEL), _F32),
        pltpu.VMEM((n_rows, D_MODEL), _BF16),
    ]
    body = functools.partial(_sample_kernel, n_streams=n_streams, rows=rows, final_norm=final_norm)
    return pl.pallas_call(
        body, grid=(N_FF_CHUNKS,), in_specs=in_specs, out_specs=out_specs, out_shape=out_shape,
        scratch_shapes=scratch_shapes, compiler_params=_compiler_params(), name="sample_layer",
    )(x, pool_h, conv_h, ffn_h, ffn_h, p.g1, p.w_in, p.w_s, p.bias, p.w_pool, p.pscale, p.w_conv, p.w_out, p.g2,
      p.w_up, p.w_up, p.w_fconv, p.w_fconv, p.b_fconv, p.b_fconv, p.w_down, p.gf)


def _prompt_call(layer, x, pool_h, conv_h, ffn_h, p, weights_b, *, n_groups, rows, n_sub, steps_per_group,
                 final_norm):
    block_rows = n_sub * rows
    w_in_b, w_out_b, w_upg_b, w_upv_b, w_down_b = weights_b

    def rows_spec(width):
        return pl.BlockSpec((block_rows, width), lambda t: (t, 0))

    def state_spec(pad, width):
        return pl.BlockSpec((1, pad, width), lambda t: (t // steps_per_group, 0, 0))

    def resident(shape):
        return pl.BlockSpec(shape, lambda t: (0,) * len(shape), pipeline_mode=pl.Buffered(1))

    def layer_resident(shape):
        return pl.BlockSpec((None,) + shape, lambda t: (layer,) + (0,) * len(shape),
                            pipeline_mode=pl.Buffered(1))

    in_specs = [
        rows_spec(D_MODEL),
        state_spec(POOL_HIST_PAD, D_B), state_spec(CONV_HIST_PAD, D_C), state_spec(CONV_HIST_PAD, 2 * D_FF),
        layer_resident((1, D_MODEL)), resident((D_MODEL, D_IN)),
        layer_resident((N_A_HEADS, GMLP_CHUNK, GMLP_CHUNK)), layer_resident((GMLP_CHUNK, D_A)),
        layer_resident((D_B, D_B)), layer_resident((1, D_B)), layer_resident((3, D_C)),
        resident((D_MODEL, D_MODEL)), layer_resident((1, D_MODEL)),
        resident((D_MODEL, D_FF)), resident((D_MODEL, D_FF)),
        layer_resident((3, 2 * D_FF)), layer_resident((1, 2 * D_FF)),
        resident((D_FF, D_MODEL)),
        resident((1, D_MODEL)),
    ]
    out_shape = [
        jax.ShapeDtypeStruct(x.shape, _F32),
        jax.ShapeDtypeStruct((n_groups, POOL_HIST_PAD, D_B), _F32),
        jax.ShapeDtypeStruct((n_groups, CONV_HIST_PAD, D_C), _F32),
        jax.ShapeDtypeStruct((n_groups, CONV_HIST_PAD, 2 * D_FF), _F32),
    ]
    out_specs = [rows_spec(D_MODEL), state_spec(POOL_HIST_PAD, D_B), state_spec(CONV_HIST_PAD, D_C),
                 state_spec(CONV_HIST_PAD, 2 * D_FF)]
    scratch_shapes = [
        pltpu.VMEM((1, POOL_HIST_PAD, D_B), _F32),
        pltpu.VMEM((1, D_B // LANES, 2, SUBLANES + POOL_HIST_PAD + rows, LANES), _F32),
        pltpu.VMEM((1, D_C // LANES, CONV_HIST_PAD + rows, LANES), _F32),
        pltpu.VMEM((1, 2 * D_FF // LANES, CONV_HIST_PAD + rows, LANES), _F32),
    ]
    body = functools.partial(_prompt_kernel, rows=rows, n_sub=n_sub, steps_per_group=steps_per_group,
                             final_norm=final_norm)
    return pl.pallas_call(
        body, grid=(n_groups * steps_per_group,), in_specs=in_specs, out_specs=out_specs, out_shape=out_shape,
        scratch_shapes=scratch_shapes, compiler_params=_compiler_params(), name="prompt_layer",
    )(x, pool_h, conv_h, ffn_h, p.g1, w_in_b, p.w_s, p.bias, p.w_pool, p.pscale, p.w_conv, w_out_b, p.g2,
      w_upg_b, w_upv_b, p.w_fconv, p.b_fconv, w_down_b, p.gf)


def _pad_front(a, pad):
    return jnp.pad(a, ((0, 0),) * (a.ndim - 2) + ((pad - a.shape[-2], 0), (0, 0)))


def kernel(x_prompt, x_sample, state_pool, state_conv, state_ffn_conv, norm1_g, w_in, w_s, b_s, w_pool,
           pool_scale, w_conv, w_out, norm2_g, w_up, w_fconv, b_fconv, w_down, final_g):
    depth = w_in.shape[0]
    batch, seq, _ = x_prompt.shape
    dec_batch, dec_seq, _ = x_sample.shape

    group_eye = jnp.eye(len(POOL_WINDOWS), dtype=bool)[None, :, None, :, None]
    w_pool_bd = jnp.where(group_eye, w_pool[:, :, :, None, :], 0.0).reshape(depth, D_B, D_B)
    bias = jnp.repeat(jnp.swapaxes(b_s, 1, 2), A_HEAD_DIM, axis=-1)
    p = types.SimpleNamespace(
        g1=norm1_g[:, None, :], w_in=w_in, w_s=w_s, bias=bias, w_pool=w_pool_bd.astype(_BF16),
        pscale=pool_scale[:, None, :], w_conv=w_conv, w_out=w_out, g2=norm2_g[:, None, :], w_up=w_up,
        w_fconv=w_fconv, b_fconv=b_fconv[:, None, :], w_down=w_down, gf=final_g[None, :])

    pool_hs = _pad_front(state_pool, POOL_HIST_PAD)
    conv_hs = _pad_front(state_conv, CONV_HIST_PAD)
    ffn_hs = _pad_front(state_ffn_conv, CONV_HIST_PAD)
    zero_pool = jnp.zeros((batch, POOL_HIST_PAD, D_B), _F32)
    zero_conv = jnp.zeros((batch, CONV_HIST_PAD, D_C), _F32)
    zero_ffn = jnp.zeros((batch, CONV_HIST_PAD, 2 * D_FF), _F32)

    xp = x_prompt.reshape(batch * seq, D_MODEL)
    xs = x_sample.reshape(dec_batch * dec_seq, D_MODEL)
    outs_p, outs_s = [], []
    for layer in range(depth):
        last = layer == depth - 1
        xs, pool_o, conv_o, ffn_g, ffn_v, v_o, *weights_b = _sample_call(
            layer, xs, pool_hs[layer], conv_hs[layer], ffn_hs[layer], p, n_streams=dec_batch, rows=dec_seq,
            final_norm=last)
        outs_s.append((pool_o, conv_o, jnp.concatenate([ffn_g, ffn_v], axis=-1), v_o))
        xp, *st_p = _prompt_call(layer, xp, zero_pool, zero_conv, zero_ffn, p, weights_b, n_groups=batch,
                                 rows=PROMPT_TILE, n_sub=PROMPT_TILES_PER_STEP,
                                 steps_per_group=seq // (PROMPT_TILE * PROMPT_TILES_PER_STEP), final_norm=last)
        outs_p.append(st_p)

    def stacked(outs, i, keep):
        return jnp.stack([o[i] for o in outs])[:, :, -keep:, :]

    return (xp.reshape(batch, seq, D_MODEL), xs.reshape(dec_batch, dec_seq, D_MODEL),
            stacked(outs_p, 0, POOL_HIST), stacked(outs_p, 1, CONV_HIST), stacked(outs_p, 2, CONV_HIST),
            stacked(outs_s, 0, POOL_HIST), stacked(outs_s, 1, CONV_HIST), stacked(outs_s, 2, CONV_HIST),
            jnp.stack([o[3] for o in outs_s]).reshape(depth, dec_batch, dec_seq, D_A))
```

```python
import functools
import types

import jax
import jax.numpy as jnp
from jax import lax
from jax.experimental import pallas as pl
from jax.experimental.pallas import tpu as pltpu

D_MODEL = 1024
D_A = 512
N_A_HEADS = 4
A_HEAD_DIM = 128
GMLP_CHUNK = 128
D_B = 256
POOL_WINDOWS = (2, 4, 8, 16)
POOL_GROUP = 64
POOL_HIST = 15
D_C = 256
CONV_HIST = 2
D_IN = 2 * D_A + D_B + 3 * D_C
D_FF = 11 * D_MODEL // 4
EPS = 1e-6
PAST_LEN = 4096

LANES = 128
SUBLANES = 8
POOL_HIST_PAD = 16
CONV_HIST_PAD = SUBLANES
FF_CHUNK = 256
N_FF_CHUNKS = D_FF // FF_CHUNK
PROMPT_TILE = 256
PROMPT_TILES_PER_STEP = 2
FF_LOOKAHEAD = 2
MIXER_PIECES_AT_CHUNK = {0: 1, 1: 1, 2: 1, 4: 1, 5: 1, 7: 1, 9: 1}
VMEM_LIMIT_BYTES = 56 * 1024 * 1024

_F32 = jnp.float32
_BF16 = jnp.bfloat16


def _rms_scale(x):
    return lax.rsqrt(jnp.mean(x * x, axis=-1, keepdims=True) + EPS)


def _rmsnorm(x, g):
    return x * _rms_scale(x) * g


def _silu_gate(gate, val):
    return gate * (1.0 / (1.0 + jnp.exp(-gate))) * val


def _causal_conv3_ref(buf, cur, w):
    rows = cur.shape[0]
    buf[CONV_HIST_PAD:, :] = cur
    out = (w[0:1] * buf[CONV_HIST_PAD - 2:CONV_HIST_PAD - 2 + rows, :]
           + w[1:2] * buf[CONV_HIST_PAD - 1:CONV_HIST_PAD - 1 + rows, :] + w[2:3] * cur)
    buf[0:CONV_HIST_PAD, :] = cur[rows - CONV_HIST_PAD:]
    return out


def _pool_window_sums(buf, hist, p, wide):
    rows = p.shape[0]
    n = POOL_HIST_PAD + rows
    a, b = buf.at[0], buf.at[1]
    a[SUBLANES:SUBLANES + POOL_HIST_PAD, :] = hist
    a[SUBLANES + POOL_HIST_PAD:, :] = p
    s2 = a[SUBLANES:SUBLANES + n, :] + a[SUBLANES - 1:SUBLANES - 1 + n, :]
    b[SUBLANES:SUBLANES + n, :] = s2
    s4 = s2 + b[SUBLANES - 2:SUBLANES - 2 + n, :]
    if not wide:
        return s2[POOL_HIST_PAD:], s4[POOL_HIST_PAD:]
    a[SUBLANES:SUBLANES + n, :] = s4
    s8 = s4 + a[SUBLANES - 4:SUBLANES - 4 + n, :]
    s16 = s8[POOL_HIST_PAD:] + s8[POOL_HIST_PAD - 8:n - 8]
    return s8[POOL_HIST_PAD:], s16


def _pool_delta(buf, hist, p, first_pos):
    rows = p.shape[0]
    low = lax.broadcasted_iota(jnp.int32, (rows, LANES), 1) < POOL_GROUP
    s2, s4 = _pool_window_sums(buf.at[0], hist[:, :LANES], p[:, :LANES], wide=False)
    s8, s16 = _pool_window_sums(buf.at[1], hist[:, LANES:], p[:, LANES:], wide=True)
    sums = jnp.concatenate([jnp.where(low, s2, s4), jnp.where(low, s8, s16)], axis=1)
    lane_b = lax.broadcasted_iota(jnp.int32, (rows, D_B), 1)
    window = jnp.where(lane_b < POOL_GROUP, POOL_WINDOWS[0],
                       jnp.where(lane_b < 2 * POOL_GROUP, POOL_WINDOWS[1],
                                 jnp.where(lane_b < 3 * POOL_GROUP, POOL_WINDOWS[2], POOL_WINDOWS[3])))
    pos1 = lax.broadcasted_iota(jnp.int32, (rows, D_B), 0) + (first_pos + 1)
    cnt = jnp.minimum(pos1, window).astype(_F32)
    return sums / cnt - p


def _init_mixer_scratch(m, n_streams):
    for s in range(n_streams):
        for j in range(D_C // LANES):
            m.conv_buf[s, j, 0:CONV_HIST_PAD, :] = m.convh_ref[s, :, j * LANES:(j + 1) * LANES]
        for j in range(D_B // LANES):
            for k in range(2):
                m.pool_buf[s, j, k, 0:SUBLANES, :] = jnp.zeros((SUBLANES, LANES), _F32)


def _mixer_half(x, m, out, *, n_streams, rows, first_pos):
    tile_rows = n_streams * rows

    def stream_rows(a, s):
        return a[s * rows:(s + 1) * rows]

    def per_stream(fn):
        parts = [fn(s) for s in range(n_streams)]
        return parts[0] if n_streams == 1 else jnp.concatenate(parts, axis=0)

    h = (x * m.g1_ref[...]).astype(_BF16)
    r1 = _rms_scale(x)
    yield
    z_uv = jnp.dot(h, m.w_in(0, 2 * D_A), preferred_element_type=_F32) * r1
    u = z_uv[:, 0:D_A]
    v = z_uv[:, D_A:]
    if m.store_v is not None:
        m.store_v(v)
    yield
    z_r = jnp.dot(h, m.w_in(2 * D_A, D_IN), preferred_element_type=_F32) * r1
    p = z_r[:, 0:D_B]
    gate_b = z_r[:, D_B:D_B + D_C]
    gate_c = z_r[:, D_B + D_C:D_B + 2 * D_C]
    h_c = z_r[:, D_B + 2 * D_C:]
    yield

    chunk = min(rows, GMLP_CHUNK)
    n_chunks = tile_rows // chunk
    v_b = v.astype(_BF16)
    tril = (lax.broadcasted_iota(jnp.int32, (chunk, chunk), 0)
            >= lax.broadcasted_iota(jnp.int32, (chunk, chunk), 1)).astype(_F32)
    y_a_heads = []
    for hd in range(N_A_HEADS):
        cols = slice(hd * A_HEAD_DIM, (hd + 1) * A_HEAD_DIM)
        w_hd = (m.w_s_ref[hd, 0:chunk, 0:chunk] * tril).astype(_BF16)
        v_hd = jnp.concatenate([v_b[c * chunk:(c + 1) * chunk, cols] for c in range(n_chunks)], axis=1)
        s_hd = jnp.dot(w_hd, v_hd, preferred_element_type=_F32)
        b_hd = m.bias_ref[0:chunk, cols]
        s_rows = jnp.concatenate(
            [s_hd[:, c * A_HEAD_DIM:(c + 1) * A_HEAD_DIM] + b_hd for c in range(n_chunks)], axis=0)
        y_a_heads.append(u[:, cols] * s_rows)
    y_a = jnp.concatenate(y_a_heads, axis=1)
    yield

    def pool_stream(s):
        p_s = stream_rows(p, s)
        d_s = _pool_delta(m.pool_buf.at[s], m.pool_hist(s), p_s, first_pos)
        m.store_pool_hist(s, p_s[rows - POOL_HIST_PAD:])
        return d_s

    d = per_stream(pool_stream)
    y_b = jnp.dot(d.astype(_BF16), m.w_pool_ref[...], preferred_element_type=_F32) * m.pscale_ref[...]
    yield

    q = gate_c * h_c
    conv_tiles = []
    for j in range(D_C // LANES):
        lanes = slice(j * LANES, (j + 1) * LANES)

        def conv_stream(s, j=j, lanes=lanes):
            q_s = stream_rows(q, s)[:, lanes]
            m.convo_ref[s, :, lanes] = q_s[rows - CONV_HIST_PAD:]
            return _causal_conv3_ref(m.conv_buf.at[s, j], q_s, m.w_conv_ref[:, lanes])

        conv_tiles.append(per_stream(conv_stream))
    y_c = gate_b * jnp.concatenate(conv_tiles, axis=1)
    y_mix = jnp.concatenate([y_a, y_b, y_c], axis=1).astype(_BF16)
    x1 = x + jnp.dot(y_mix, m.w_out(), preferred_element_type=_F32)
    yield
    out["x1"] = x1
    out["h2"] = _rmsnorm(x1, m.g2_ref[...]).astype(_BF16)


def _prompt_kernel(x_ref, poolh_ref, convh_ref, ffnh_ref, g1_ref, w_in_hbm, w_s_ref, bias_ref, w_pool_ref,
                   pscale_ref, w_conv_ref, w_out_hbm, g2_ref, w_upg_hbm, w_upv_hbm, w_fconv_ref, b_fconv_ref,
                   w_down_hbm, gf_ref, y_ref, poolo_ref, convo_ref, ffno_ref, pool_s, pool_buf, conv_buf,
                   ffn_buf, w_in_ref, w_out_ref, w_upg_ref, w_upv_ref, w_down_ref, w_sem,
                   *, rows, n_sub, steps_per_group, final_norm):
    seq_step = lax.rem(pl.program_id(0), steps_per_group)

    @pl.when(pl.program_id(0) == 0)
    def _load_weights():
        copies = [pltpu.make_async_copy(src, dst, w_sem.at[i]) for i, (src, dst) in enumerate(
            ((w_in_hbm, w_in_ref), (w_out_hbm, w_out_ref), (w_upg_hbm, w_upg_ref), (w_upv_hbm, w_upv_ref),
             (w_down_hbm, w_down_ref)))]
        for copy in copies:
            copy.start()
        for copy in copies:
            copy.wait()

    def store_pool_hist(s, hist):
        pool_s[s] = hist
        poolo_ref[s] = hist

    m = types.SimpleNamespace(
        g1_ref=g1_ref, w_in=lambda lo, hi: w_in_ref[:, lo:hi], w_s_ref=w_s_ref, bias_ref=bias_ref,
        w_pool_ref=w_pool_ref, pscale_ref=pscale_ref, w_conv_ref=w_conv_ref, w_out=lambda: w_out_ref[...],
        g2_ref=g2_ref, store_v=None, pool_buf=pool_buf, conv_buf=conv_buf, convh_ref=convh_ref,
        convo_ref=convo_ref, pool_hist=lambda s: pool_s[s], store_pool_hist=store_pool_hist)

    @pl.when(seq_step == 0)
    def _load_history():
        pool_s[...] = poolh_ref[...]
        _init_mixer_scratch(m, 1)
        for j in range(2 * D_FF // LANES):
            ffn_buf[0, j, 0:CONV_HIST_PAD, :] = ffnh_ref[0, :, j * LANES:(j + 1) * LANES]

    def ffn_half(x1, h2, between_chunks):
        def up_proj(c):
            return [jnp.dot(h2, w_ref[c], preferred_element_type=_F32) for w_ref in (w_upg_ref, w_upv_ref)]

        acc = x1
        ups_queue = [up_proj(c) for c in range(FF_LOOKAHEAD)]
        for c in range(N_FF_CHUNKS):
            between_chunks(c)
            if c + FF_LOOKAHEAD < N_FF_CHUNKS:
                ups_queue.append(up_proj(c + FF_LOOKAHEAD))
            ups = ups_queue.pop(0)
            halves = []
            for up, col0 in zip(ups, (c * FF_CHUNK, D_FF + c * FF_CHUNK)):
                tiles = []
                for j in range(col0 // LANES, (col0 + FF_CHUNK) // LANES):
                    lanes = slice(j * LANES, (j + 1) * LANES)
                    up_j = up[:, lanes.start - col0:lanes.stop - col0]
                    ffno_ref[0, :, lanes] = up_j[rows - CONV_HIST_PAD:]
                    tiles.append(_causal_conv3_ref(ffn_buf.at[0, j], up_j, w_fconv_ref[:, lanes])
                                 + b_fconv_ref[:, lanes])
                halves.append(jnp.concatenate(tiles, axis=1))
            act = _silu_gate(*halves).astype(_BF16)
            acc = acc + jnp.dot(act, w_down_ref[c * FF_CHUNK:(c + 1) * FF_CHUNK, :],
                                preferred_element_type=_F32)
        return acc

    outs = [{} for _ in range(n_sub)]
    mixers = [_mixer_half(x_ref[sub * rows:(sub + 1) * rows, :], m, outs[sub], n_streams=1, rows=rows,
                          first_pos=(seq_step * n_sub + sub) * rows) for sub in range(n_sub)]
    for _ in mixers[0]:
        pass
    for sub in range(n_sub):
        following = mixers[sub + 1] if sub + 1 < n_sub else iter(())

        def advance_following(c, following=following):
            for _ in range(MIXER_PIECES_AT_CHUNK.get(c, 0)):
                next(following, None)

        acc = ffn_half(outs[sub]["x1"], outs[sub]["h2"], advance_following)
        for _ in following:
            pass
        y_ref[sub * rows:(sub + 1) * rows, :] = _rmsnorm(acc, gf_ref[...]) if final_norm else acc


def _sample_kernel(x_ref, poolh_ref, convh_ref, ffnh_g_ref, ffnh_v_ref, g1_ref, w_in_ref, w_s_ref, bias_ref,
                   w_pool_ref, pscale_ref, w_conv_ref, w_out_ref, g2_ref, w_upg_ref, w_upv_ref, w_fcg_ref,
                   w_fcv_ref, b_fcg_ref, b_fcv_ref, w_down_ref, gf_ref,
                   y_ref, poolo_ref, convo_ref, ffno_g_ref, ffno_v_ref, v_ref, w_in_b_ref, w_out_b_ref,
                   w_upg_b_ref, w_upv_b_ref, w_down_b_ref,
                   pool_buf, conv_buf, fconv_buf, acc_s, h2_s, *, n_streams, rows, final_norm):
    c = pl.program_id(0)

    @pl.when(c == 0)
    def _mixers():
        w_in_b = w_in_ref[...].astype(_BF16)
        w_out_b = w_out_ref[...].astype(_BF16)
        w_in_b_ref[...] = w_in_b
        w_out_b_ref[...] = w_out_b

        def store_v(v):
            v_ref[...] = v

        def store_pool_hist(s, hist):
            poolo_ref[s] = hist

        m = types.SimpleNamespace(
            g1_ref=g1_ref, w_in=lambda lo, hi: w_in_b[:, lo:hi], w_s_ref=w_s_ref, bias_ref=bias_ref,
            w_pool_ref=w_pool_ref, pscale_ref=pscale_ref, w_conv_ref=w_conv_ref, w_out=lambda: w_out_b,
            g2_ref=g2_ref, store_v=store_v, pool_buf=pool_buf, conv_buf=conv_buf, convh_ref=convh_ref,
            convo_ref=convo_ref, pool_hist=lambda s: poolh_ref[s], store_pool_hist=store_pool_hist)
        _init_mixer_scratch(m, n_streams)
        out = {}
        for _ in _mixer_half(x_ref[...], m, out, n_streams=n_streams, rows=rows, first_pos=PAST_LEN):
            pass
        acc_s[...] = out["x1"]
        h2_s[...] = out["h2"]

    w_g = w_upg_ref[...].astype(_BF16)
    w_v = w_upv_ref[...].astype(_BF16)
    w_d = w_down_ref[...].astype(_BF16)
    w_upg_b_ref[...] = w_g
    w_upv_b_ref[...] = w_v
    w_down_b_ref[...] = w_d

    h2 = h2_s[...]
    halves = []
    for half, (w, hist_ref, w_fc_ref, b_fc_ref, state_ref) in enumerate((
            (w_g, ffnh_g_ref, w_fcg_ref, b_fcg_ref, ffno_g_ref),
            (w_v, ffnh_v_ref, w_fcv_ref, b_fcv_ref, ffno_v_ref))):
        up = jnp.dot(h2, w, preferred_element_type=_F32)
        tiles = []
        for j in range(FF_CHUNK // LANES):
            lanes = slice(j * LANES, (j + 1) * LANES)
            parts = []
            for s in range(n_streams):
                buf = fconv_buf.at[s, half, j]
                buf[0:CONV_HIST_PAD, :] = hist_ref[s, :, lanes]
                up_s = up[s * rows:(s + 1) * rows, lanes]
                state_ref[s, :, lanes] = up_s[rows - CONV_HIST_PAD:]
                parts.append(_causal_conv3_ref(buf, up_s, w_fc_ref[:, lanes]))
            tiles.append(jnp.concatenate(parts, axis=0) + b_fc_ref[:, lanes])
        halves.append(jnp.concatenate(tiles, axis=1))
    act = _silu_gate(*halves).astype(_BF16)
    acc_s[...] += jnp.dot(act, w_d, preferred_element_type=_F32)

    @pl.when(c == N_FF_CHUNKS - 1)
    def _finish():
        acc = acc_s[...]
        y_ref[...] = _rmsnorm(acc, gf_ref[...]) if final_norm else acc


def _compiler_params():
    return pltpu.CompilerParams(dimension_semantics=("arbitrary",), vmem_limit_bytes=VMEM_LIMIT_BYTES)


def _sample_call(layer, x, pool_h, conv_h, ffn_h, p, *, n_streams, rows, final_norm):
    n_rows = n_streams * rows

    def whole(shape):
        return pl.BlockSpec(shape, lambda c: (0,) * len(shape))

    def layer_whole(shape):
        return pl.BlockSpec((None,) + shape, lambda c: (layer,) + (0,) * len(shape), pipeline_mode=pl.Buffered(1))

    def layer_chunk(shape, axis, offset):
        def index(c):
            idx = [0] * len(shape)
            idx[axis] = c + offset
            return (layer,) + tuple(idx)
        return pl.BlockSpec((None,) + shape, index)

    def chunk(shape, axis, offset=0):
        def index(c):
            idx = [0] * len(shape)
            idx[axis] = c + offset
            return tuple(idx)
        return pl.BlockSpec(shape, index)

    hist_chunk = (n_streams, CONV_HIST_PAD, FF_CHUNK)
    up_chunk_out = pl.BlockSpec((None, D_MODEL, FF_CHUNK), lambda c: (c, 0, 0))
    in_specs = [
        whole((n_rows, D_MODEL)),
        whole((n_streams, POOL_HIST_PAD, D_B)), whole((n_streams, CONV_HIST_PAD, D_C)),
        chunk(hist_chunk, 2), chunk(hist_chunk, 2, N_FF_CHUNKS),
        layer_whole((1, D_MODEL)), layer_whole((D_MODEL, D_IN)),
        layer_whole((N_A_HEADS, GMLP_CHUNK, GMLP_CHUNK)), layer_whole((GMLP_CHUNK, D_A)),
        layer_whole((D_B, D_B)), layer_whole((1, D_B)), layer_whole((3, D_C)),
        layer_whole((D_MODEL, D_MODEL)), layer_whole((1, D_MODEL)),
        layer_chunk((D_MODEL, FF_CHUNK), 1, 0), layer_chunk((D_MODEL, FF_CHUNK), 1, N_FF_CHUNKS),
        layer_chunk((3, FF_CHUNK), 1, 0), layer_chunk((3, FF_CHUNK), 1, N_FF_CHUNKS),
        layer_chunk((1, FF_CHUNK), 1, 0), layer_chunk((1, FF_CHUNK), 1, N_FF_CHUNKS),
        layer_chunk((FF_CHUNK, D_MODEL), 0, 0),
        whole((1, D_MODEL)),
    ]
    out_shape = [
        jax.ShapeDtypeStruct((n_rows, D_MODEL), _F32),
        jax.ShapeDtypeStruct((n_streams, POOL_HIST_PAD, D_B), _F32),
        jax.ShapeDtypeStruct((n_streams, CONV_HIST_PAD, D_C), _F32),
        jax.ShapeDtypeStruct((n_streams, CONV_HIST_PAD, D_FF), _F32),
        jax.ShapeDtypeStruct((n_streams, CONV_HIST_PAD, D_FF), _F32),
        jax.ShapeDtypeStruct((n_rows, D_A), _F32),
        jax.ShapeDtypeStruct((D_MODEL, D_IN), _BF16),
        jax.ShapeDtypeStruct((D_MODEL, D_MODEL), _BF16),
        jax.ShapeDtypeStruct((N_FF_CHUNKS, D_MODEL, FF_CHUNK), _BF16),
        jax.ShapeDtypeStruct((N_FF_CHUNKS, D_MODEL, FF_CHUNK), _BF16),
        jax.ShapeDtypeStruct((D_FF, D_MODEL), _BF16),
    ]
    out_specs = [
        whole((n_rows, D_MODEL)),
        whole((n_streams, POOL_HIST_PAD, D_B)), whole((n_streams, CONV_HIST_PAD, D_C)),
        chunk(hist_chunk, 2), chunk(hist_chunk, 2),
        whole((n_rows, D_A)),
        whole((D_MODEL, D_IN)), whole((D_MODEL, D_MODEL)),
        up_chunk_out, up_chunk_out, chunk((FF_CHUNK, D_MODEL), 0),
    ]
    scratch_shapes = [
        pltpu.VMEM((n_streams, D_B // LANES, 2, SUBLANES + POOL_HIST_PAD + rows, LANES), _F32),
        pltpu.VMEM((n_streams, D_C // LANES, CONV_HIST_PAD + rows, LANES), _F32),
        pltpu.VMEM((n_streams, 2, FF_CHUNK // LANES, CONV_HIST_PAD + rows, LANES), _F32),
        pltpu.VMEM((n_rows, D_MODEL), _F32),
        pltpu.VMEM((n_rows, D_MODEL), _BF16),
    ]
    body = functools.partial(_sample_kernel, n_streams=n_streams, rows=rows, final_norm=final_norm)
    return pl.pallas_call(
        body, grid=(N_FF_CHUNKS,), in_specs=in_specs, out_specs=out_specs, out_shape=out_shape,
        scratch_shapes=scratch_shapes, compiler_params=_compiler_params(), name="sample_layer",
    )(x, pool_h, conv_h, ffn_h, ffn_h, p.g1, p.w_in, p.w_s, p.bias, p.w_pool, p.pscale, p.w_conv, p.w_out, p.g2,
      p.w_up, p.w_up, p.w_fconv, p.w_fconv, p.b_fconv, p.b_fconv, p.w_down, p.gf)


def _prompt_call(layer, x, pool_h, conv_h, ffn_h, p, weights_b, *, n_groups, rows, n_sub, steps_per_group,
                 final_norm):
    block_rows = n_sub * rows
    w_in_b, w_out_b, w_upg_b, w_upv_b, w_down_b = weights_b

    def rows_spec(width):
        return pl.BlockSpec((block_rows, width), lambda t: (t, 0))

    def state_spec(pad, width):
        return pl.BlockSpec((1, pad, width), lambda t: (t // steps_per_group, 0, 0))

    def resident(shape):
        return pl.BlockSpec(shape, lambda t: (0,) * len(shape), pipeline_mode=pl.Buffered(1))

    in_hbm = pl.BlockSpec(memory_space=pl.ANY)

    def layer_resident(shape):
        return pl.BlockSpec((None,) + shape, lambda t: (layer,) + (0,) * len(shape),
                            pipeline_mode=pl.Buffered(1))

    in_specs = [
        rows_spec(D_MODEL),
        state_spec(POOL_HIST_PAD, D_B), state_spec(CONV_HIST_PAD, D_C), state_spec(CONV_HIST_PAD, 2 * D_FF),
        layer_resident((1, D_MODEL)), in_hbm,
        layer_resident((N_A_HEADS, GMLP_CHUNK, GMLP_CHUNK)), layer_resident((GMLP_CHUNK, D_A)),
        layer_resident((D_B, D_B)), layer_resident((1, D_B)), layer_resident((3, D_C)),
        in_hbm, layer_resident((1, D_MODEL)),
        in_hbm, in_hbm,
        layer_resident((3, 2 * D_FF)), layer_resident((1, 2 * D_FF)),
        in_hbm,
        resident((1, D_MODEL)),
    ]
    out_shape = [
        jax.ShapeDtypeStruct(x.shape, _F32),
        jax.ShapeDtypeStruct((n_groups, POOL_HIST_PAD, D_B), _F32),
        jax.ShapeDtypeStruct((n_groups, CONV_HIST_PAD, D_C), _F32),
        jax.ShapeDtypeStruct((n_groups, CONV_HIST_PAD, 2 * D_FF), _F32),
    ]
    out_specs = [rows_spec(D_MODEL), state_spec(POOL_HIST_PAD, D_B), state_spec(CONV_HIST_PAD, D_C),
                 state_spec(CONV_HIST_PAD, 2 * D_FF)]
    scratch_shapes = [
        pltpu.VMEM((1, POOL_HIST_PAD, D_B), _F32),
        pltpu.VMEM((1, D_B // LANES, 2, SUBLANES + POOL_HIST_PAD + rows, LANES), _F32),
        pltpu.VMEM((1, D_C // LANES, CONV_HIST_PAD + rows, LANES), _F32),
        pltpu.VMEM((1, 2 * D_FF // LANES, CONV_HIST_PAD + rows, LANES), _F32),
        *[pltpu.VMEM(w.shape, _BF16) for w in weights_b],
        pltpu.SemaphoreType.DMA((len(weights_b),)),
    ]
    body = functools.partial(_prompt_kernel, rows=rows, n_sub=n_sub, steps_per_group=steps_per_group,
                             final_norm=final_norm)
    return pl.pallas_call(
        body, grid=(n_groups * steps_per_group,), in_specs=in_specs, out_specs=out_specs, out_shape=out_shape,
        scratch_shapes=scratch_shapes, compiler_params=_compiler_params(), name="prompt_layer",
    )(x, pool_h, conv_h, ffn_h, p.g1, w_in_b, p.w_s, p.bias, p.w_pool, p.pscale, p.w_conv, w_out_b, p.g2,
      w_upg_b, w_upv_b, p.w_fconv, p.b_fconv, w_down_b, p.gf)


def _pad_front(a, pad):
    return jnp.pad(a, ((0, 0),) * (a.ndim - 2) + ((pad - a.shape[-2], 0), (0, 0)))


def kernel(x_prompt, x_sample, state_pool, state_conv, state_ffn_conv, norm1_g, w_in, w_s, b_s, w_pool,
           pool_scale, w_conv, w_out, norm2_g, w_up, w_fconv, b_fconv, w_down, final_g):
    depth = w_in.shape[0]
    batch, seq, _ = x_prompt.shape
    dec_batch, dec_seq, _ = x_sample.shape

    group_eye = jnp.eye(len(POOL_WINDOWS), dtype=bool)[None, :, None, :, None]
    w_pool_bd = jnp.where(group_eye, w_pool[:, :, :, None, :], 0.0).reshape(depth, D_B, D_B)
    bias = jnp.repeat(jnp.swapaxes(b_s, 1, 2), A_HEAD_DIM, axis=-1)
    p = types.SimpleNamespace(
        g1=norm1_g[:, None, :], w_in=w_in, w_s=w_s, bias=bias, w_pool=w_pool_bd.astype(_BF16),
        pscale=pool_scale[:, None, :], w_conv=w_conv, w_out=w_out, g2=norm2_g[:, None, :], w_up=w_up,
        w_fconv=w_fconv, b_fconv=b_fconv[:, None, :], w_down=w_down, gf=final_g[None, :])

    pool_hs = _pad_front(state_pool, POOL_HIST_PAD)
    conv_hs = _pad_front(state_conv, CONV_HIST_PAD)
    ffn_hs = _pad_front(state_ffn_conv, CONV_HIST_PAD)
    zero_pool = jnp.zeros((batch, POOL_HIST_PAD, D_B), _F32)
    zero_conv = jnp.zeros((batch, CONV_HIST_PAD, D_C), _F32)
    zero_ffn = jnp.zeros((batch, CONV_HIST_PAD, 2 * D_FF), _F32)

    xp = x_prompt.reshape(batch * seq, D_MODEL)
    xs = x_sample.reshape(dec_batch * dec_seq, D_MODEL)
    outs_p, outs_s = [], []
    for layer in range(depth):
        last = layer == depth - 1
        xs, pool_o, conv_o, ffn_g, ffn_v, v_o, *weights_b = _sample_call(
            layer, xs, pool_hs[layer], conv_hs[layer], ffn_hs[layer], p, n_streams=dec_batch, rows=dec_seq,
            final_norm=last)
        outs_s.append((pool_o, conv_o, jnp.concatenate([ffn_g, ffn_v], axis=-1), v_o))
        xp, *st_p = _prompt_call(layer, xp, zero_pool, zero_conv, zero_ffn, p, weights_b, n_groups=batch,
                                 rows=PROMPT_TILE, n_sub=PROMPT_TILES_PER_STEP,
                                 steps_per_group=seq // (PROMPT_TILE * PROMPT_TILES_PER_STEP), final_norm=last)
        outs_p.append(st_p)

    def stacked(outs, i, keep):
        return jnp.stack([o[i] for o in outs])[:, :, -keep:, :]

    return (xp.reshape(batch, seq, D_MODEL), xs.reshape(dec_batch, dec_seq, D_MODEL),
            stacked(outs_p, 0, POOL_HIST), stacked(outs_p, 1, CONV_HIST), stacked(outs_p, 2, CONV_HIST),
            stacked(outs_s, 0, POOL_HIST), stacked(outs_s, 1, CONV_HIST), stacked(outs_s, 2, CONV_HIST),
            jnp.stack([o[3] for o in outs_s]).reshape(depth, dec_batch, dec_seq, D_A))
```
